```python
import jax, jax.numpy as jnp
from jax import lax
import numpy as np

D_MODEL = 1024
BATCH = 2
SEQ = 16384
DEPTH = 2

GRID_W = 64
CTX_LEN = 256
HEAD_DIM = 64
ATTN_Q_HEADS = 8
ATTN_KV_HEADS = 2
GQA_GROUP = ATTN_Q_HEADS // ATTN_KV_HEADS
WINDOW = 128
BLOCK = 128
ATTN_DIM = ATTN_Q_HEADS * HEAD_DIM
KV_DIM = ATTN_KV_HEADS * HEAD_DIM
GM_GROUPS = 8
GM_DIM = GM_GROUPS * HEAD_DIM
CHUNK = 128
IN_DIM_EVEN = ATTN_DIM + 2 * KV_DIM + 2 * GM_DIM
MIX_DIM_EVEN = ATTN_DIM + GM_DIM
SPLITS_EVEN = [ATTN_DIM, ATTN_DIM + KV_DIM, ATTN_DIM + 2 * KV_DIM, ATTN_DIM + 2 * KV_DIM + GM_DIM]
CONV_DIM = D_MODEL
CONV_WIDTH = 3
D_FF = -(-8 * D_MODEL // (3 * 256)) * 256
ROPE_THETA = 10000.0
RMS_EPS = 1e-6
LN_EPS = 1e-5
NEG_INF = -1e30
N_EVEN = (DEPTH + 1) // 2
N_ODD = DEPTH // 2

kernel_name = "hybrid_swa_gmlp_shortconv_dit"


def rms_norm(x, g):
    xf = x.astype(jnp.float32)
    y = xf * lax.rsqrt(jnp.mean(xf * xf, axis=-1, keepdims=True) + RMS_EPS)
    return (y * g.astype(jnp.float32)).astype(x.dtype)


def modulate(x, g, shift, scale):
    return rms_norm(x, g) * (1.0 + scale) + shift


def axial_rope_tables(rows):
    q = HEAD_DIM // 4
    inv = ROPE_THETA ** (-jnp.arange(q, dtype=jnp.float32) / q)
    row = jnp.repeat(jnp.arange(rows, dtype=jnp.float32), GRID_W)
    col = jnp.tile(jnp.arange(GRID_W, dtype=jnp.float32), rows)
    ang = jnp.stack([row[:, None] * inv, col[:, None] * inv], axis=1)
    return jnp.cos(ang), jnp.sin(ang)


def apply_axial_rope(x, cos, sin):
    q = HEAD_DIM // 4
    xf = x.astype(jnp.float32).reshape(*x.shape[:-1], 2, 2, q)
    a, b = xf[..., 0, :], xf[..., 1, :]
    cc, ss = cos[None, :, None], sin[None, :, None]
    out = jnp.stack([a * cc - b * ss, a * ss + b * cc], axis=-2)
    return out.reshape(x.shape).astype(x.dtype)


def window_attention(q, k, v, kc, vc, sink):
    B, S = q.shape[0], q.shape[1]
    nb = S // BLOCK
    scale = HEAD_DIM ** -0.5
    qb = q.reshape(B, nb, BLOCK, ATTN_KV_HEADS, GQA_GROUP, HEAD_DIM)
    pad = ((0, 0), (BLOCK, BLOCK), (0, 0), (0, 0))
    kp = jnp.pad(k, pad).reshape(B, nb + 2, BLOCK, ATTN_KV_HEADS, HEAD_DIM)
    vp = jnp.pad(v, pad).reshape(B, nb + 2, BLOCK, ATTN_KV_HEADS, HEAD_DIM)
    kw = jnp.concatenate([kp[:, :-2], kp[:, 1:-1], kp[:, 2:]], axis=2)
    vw = jnp.concatenate([vp[:, :-2], vp[:, 1:-1], vp[:, 2:]], axis=2)
    s_win = jnp.einsum('bnqkgd,bnrkd->bnkgqr', qb, kw, preferred_element_type=jnp.float32) * scale
    qpos = jnp.arange(nb)[:, None] * BLOCK + jnp.arange(BLOCK)[None, :]
    kpos = jnp.arange(nb)[:, None] * BLOCK + jnp.arange(3 * BLOCK)[None, :] - BLOCK
    valid = (jnp.abs(qpos[:, :, None] - kpos[:, None, :]) <= WINDOW) \
        & (kpos >= 0)[:, None, :] & (kpos < S)[:, None, :]
    s_win = jnp.where(valid[None, :, None, None], s_win, NEG_INF)
    s_ctx = jnp.einsum('bnqkgd,blkd->bnkgql', qb, kc, preferred_element_type=jnp.float32) * scale
    s_sink = jnp.broadcast_to(
        sink.astype(jnp.float32).reshape(ATTN_KV_HEADS, GQA_GROUP)[None, None, :, :, None, None],
        s_win.shape[:-1] + (1,))
    p = jax.nn.softmax(jnp.concatenate([s_win, s_ctx, s_sink], axis=-1), axis=-1)
    nw = 3 * BLOCK
    L = kc.shape[1]
    p_win = p[..., :nw].astype(v.dtype)
    p_ctx = p[..., nw:nw + L].astype(v.dtype)
    o = jnp.einsum('bnkgqr,bnrkd->bnqkgd', p_win, vw) + jnp.einsum('bnkgql,blkd->bnqkgd', p_ctx, vc)
    return o.reshape(B, S, ATTN_DIM)


def context_attention(qc, kc, vc, sink):
    B, L = qc.shape[0], qc.shape[1]
    qg = qc.reshape(B, L, ATTN_KV_HEADS, GQA_GROUP, HEAD_DIM)
    s = jnp.einsum('blkgd,bmkd->bkglm', qg, kc, preferred_element_type=jnp.float32) * HEAD_DIM ** -0.5
    s_sink = jnp.broadcast_to(
        sink.astype(jnp.float32).reshape(ATTN_KV_HEADS, GQA_GROUP)[None, :, :, None, None],
        s.shape[:-1] + (1,))
    p = jax.nn.softmax(jnp.concatenate([s, s_sink], axis=-1), axis=-1)[..., :L].astype(vc.dtype)
    o = jnp.einsum('bkglm,bmkd->blkgd', p, vc)
    return o.reshape(B, L, ATTN_DIM)


def chunk_gating(u, v, v_norm, ws, bs):
    B, N, _ = v.shape
    nc = N // CHUNK
    vf = v.astype(jnp.float32).reshape(B, N, GM_GROUPS, HEAD_DIM)
    mu = jnp.mean(vf, axis=-1, keepdims=True)
    var = jnp.mean(jnp.square(vf - mu), axis=-1, keepdims=True)
    vn = ((vf - mu) * lax.rsqrt(var + LN_EPS) * v_norm.astype(jnp.float32).reshape(GM_GROUPS, HEAD_DIM)).astype(v.dtype)
    vn = vn.reshape(B, nc, CHUNK, GM_GROUPS, HEAD_DIM)
    s = jnp.einsum('gij,bnjgd->bnigd', ws, vn) + bs.T[None, None, :, :, None]
    return u * s.reshape(B, N, GM_DIM)


def even_mixer(h, hc, w_in, sink, v_norm, ws, bs, w_out, cos, sin, ctx_out):
    B, S, _ = h.shape
    L = hc.shape[1]
    q, k, v, u, gv = jnp.split(h @ w_in, SPLITS_EVEN, axis=-1)
    q = apply_axial_rope(q.reshape(B, S, ATTN_Q_HEADS, HEAD_DIM), cos, sin)
    k = apply_axial_rope(k.reshape(B, S, ATTN_KV_HEADS, HEAD_DIM), cos, sin)
    v = v.reshape(B, S, ATTN_KV_HEADS, HEAD_DIM)
    if ctx_out:
        qc, kc, vc, uc, gvc = jnp.split(hc @ w_in, SPLITS_EVEN, axis=-1)
    else:
        kc, vc = jnp.split(hc @ w_in[:, ATTN_DIM:ATTN_DIM + 2 * KV_DIM], 2, axis=-1)
    kc = kc.reshape(B, L, ATTN_KV_HEADS, HEAD_DIM)
    vc = vc.reshape(B, L, ATTN_KV_HEADS, HEAD_DIM)
    o_attn = window_attention(q, k, v, kc, vc, sink)
    o_gm = chunk_gating(u, gv, v_norm, ws, bs)
    y = jnp.concatenate([o_attn, o_gm], axis=-1) @ w_out
    yc = None
    if ctx_out:
        oc_attn = context_attention(qc.reshape(B, L, ATTN_Q_HEADS, HEAD_DIM), kc, vc, sink)
        oc_gm = chunk_gating(uc, gvc, v_norm, ws, bs)
        yc = jnp.concatenate([oc_attn, oc_gm], axis=-1) @ w_out
    return y, yc


def short_conv_mixer(h, w_in, conv_w, w_out):
    bg, cg, hx = jnp.split(h @ w_in, 3, axis=-1)
    y = cg * hx
    yconv = lax.conv_general_dilated(
        y, conv_w[:, None, :].astype(y.dtype), window_strides=(1,),
        padding=[(CONV_WIDTH // 2, CONV_WIDTH // 2)],
        dimension_numbers=('NWC', 'WIO', 'NWC'), feature_group_count=CONV_DIM)
    return (bg * yconv) @ w_out


def swiglu(h, w1, w3, w2):
    return (jax.nn.silu(h @ w1) * (h @ w3)) @ w2


def setup_inputs(seed: int = 0) -> dict:
    key = jax.random.key(seed)
    ks = jax.random.split(key, 24)
    f32 = jnp.float32
    D = D_MODEL
    nrm = lambda k, shape, s: jax.random.normal(k, shape, f32) * s
    return {
        "x": nrm(ks[0], (BATCH, SEQ, D), 1.0),
        "c": nrm(ks[1], (BATCH, D), 1.0),
        "ctx": nrm(ks[2], (BATCH, CTX_LEN, D), 1.0),
        "c_ctx": nrm(ks[3], (D,), 1.0),
        "w_mod": nrm(ks[4], (DEPTH, D, 6 * D), D ** -0.5),
        "b_mod": nrm(ks[5], (DEPTH, 6 * D), 0.02),
        "g_mix_pre": 1.0 + nrm(ks[6], (DEPTH, D), 0.05),
        "g_mix_post": 1.0 + nrm(ks[7], (DEPTH, D), 0.05),
        "g_ffn_pre": 1.0 + nrm(ks[8], (DEPTH, D), 0.05),
        "g_ffn_post": 1.0 + nrm(ks[9], (DEPTH, D), 0.05),
        "ffn_w1": nrm(ks[10], (DEPTH, D, D_FF), D ** -0.5),
        "ffn_w3": nrm(ks[11], (DEPTH, D, D_FF), D ** -0.5),
        "ffn_w2": nrm(ks[12], (DEPTH, D_FF, D), D_FF ** -0.5),
        "a_w_in": nrm(ks[13], (N_EVEN, D, IN_DIM_EVEN), D ** -0.5),
        "a_sink": nrm(ks[14], (N_EVEN, ATTN_Q_HEADS), 0.5),
        "gm_v_norm": 1.0 + nrm(ks[15], (N_EVEN, GM_DIM), 0.05),
        "gm_ws": nrm(ks[16], (N_EVEN, GM_GROUPS, CHUNK, CHUNK), CHUNK ** -0.5),
        "gm_bs": 1.0 + nrm(ks[17], (N_EVEN, GM_GROUPS, CHUNK), 0.1),
        "a_w_out": nrm(ks[18], (N_EVEN, MIX_DIM_EVEN, D), MIX_DIM_EVEN ** -0.5),
        "sc_w_in": nrm(ks[19], (N_ODD, D, 3 * CONV_DIM), D ** -0.5),
        "sc_conv": nrm(ks[20], (N_ODD, CONV_WIDTH, CONV_DIM), CONV_WIDTH ** -0.5),
        "sc_w_out": nrm(ks[21], (N_ODD, CONV_DIM, D), CONV_DIM ** -0.5),
    }


def reference(x, c, ctx, c_ctx, w_mod, b_mod, g_mix_pre, g_mix_post, g_ffn_pre, g_ffn_post,
              ffn_w1, ffn_w3, ffn_w2, a_w_in, a_sink, gm_v_norm, gm_ws, gm_bs, a_w_out,
              sc_w_in, sc_conv, sc_w_out):
    n_tok = x.shape[1]
    rows = n_tok // GRID_W
    cos, sin = axial_rope_tables(rows)
    silu_c = jax.nn.silu(c)
    silu_cc = jax.nn.silu(c_ctx)
    xc = ctx
    for i in range(DEPTH):
        ctx_out = any(j % 2 == 0 for j in range(i + 1, DEPTH))
        ctx_in = (i % 2 == 0) or ctx_out
        mod = (silu_c @ w_mod[i] + b_mod[i])[:, None, :]
        sh_m, sc_m, gt_m, sh_f, sc_f, gt_f = jnp.split(mod, 6, axis=-1)
        h = modulate(x, g_mix_pre[i], sh_m, sc_m)
        hc = None
        if ctx_in:
            mod_c = silu_cc @ w_mod[i] + b_mod[i]
            csh_m, csc_m, cgt_m, csh_f, csc_f, cgt_f = jnp.split(mod_c, 6, axis=-1)
            hc = modulate(xc, g_mix_pre[i], csh_m, csc_m)
        if i % 2 == 0:
            e = i // 2
            y, yc = even_mixer(h, hc, a_w_in[e], a_sink[e], gm_v_norm[e], gm_ws[e], gm_bs[e],
                               a_w_out[e], cos, sin, ctx_out)
        else:
            o = i // 2
            y = short_conv_mixer(h, sc_w_in[o], sc_conv[o], sc_w_out[o])
            yc = short_conv_mixer(hc, sc_w_in[o], sc_conv[o], sc_w_out[o]) if ctx_out else None
        x = x + gt_m * rms_norm(y, g_mix_post[i])
        hf = modulate(x, g_ffn_pre[i], sh_f, sc_f)
        x = x + gt_f * rms_norm(swiglu(hf, ffn_w1[i], ffn_w3[i], ffn_w2[i]), g_ffn_post[i])
        if ctx_out:
            xc = xc + cgt_m * rms_norm(yc, g_mix_post[i])
            hcf = modulate(xc, g_ffn_pre[i], csh_f, csc_f)
            xc = xc + cgt_f * rms_norm(swiglu(hcf, ffn_w1[i], ffn_w3[i], ffn_w2[i]), g_ffn_post[i])
    return x
```

```python
import functools

import jax
import jax.numpy as jnp
from jax import lax
from jax.experimental import pallas as pl
from jax.experimental.pallas import tpu as pltpu

D_MODEL = 1024
DEPTH = 2
GRID_W = 64
HEAD_DIM = 64
Q_HEADS = 8
KV_HEADS = 2
GQA = Q_HEADS // KV_HEADS
WINDOW = 128
BLOCK = 128
ATTN_DIM = Q_HEADS * HEAD_DIM
KV_DIM = KV_HEADS * HEAD_DIM
GM_GROUPS = 8
GM_DIM = GM_GROUPS * HEAD_DIM
CHUNK = 128
IN_DIM_EVEN = ATTN_DIM + 2 * KV_DIM + 2 * GM_DIM
CONV_WIDTH = 3
D_FF = 2816
ROPE_THETA = 10000.0
RMS_EPS = 1e-6
LN_EPS = 1e-5
NEG_INF = -1e30
N_MOD = 6

LANES = 128
SUBLANES_F32 = 8
MXU_WIDTH = 256
VMEM_LIMIT_BYTES = 56 * 1024 * 1024

TOKENS_PER_STEP = 512
FF_CHUNK = MXU_WIDTH
MOD_ROWS = 8

F32 = jnp.float32
BF16 = jnp.bfloat16


def _dot(a, b):
    return jnp.dot(a, b, preferred_element_type=F32)


def _dot_nt(a, b):
    return lax.dot_general(a, b, (((1,), (1,)), ((), ())), preferred_element_type=F32)


def _rms_norm(x, g):
    ms = jnp.mean(x * x, axis=-1, keepdims=True)
    return x * lax.rsqrt(ms + RMS_EPS) * g


def _modulate(x, g, shift, scale):
    return _rms_norm(x, g) * (1.0 + scale) + shift


def _silu(x):
    return x * jax.nn.sigmoid(x)


def _split_bf16(x):
    hi = x.astype(BF16)
    lo = (x - hi.astype(F32)).astype(BF16)
    return hi, lo


def _resident(shape):
    nd = len(shape)
    return pl.BlockSpec(shape, lambda *_: (0,) * nd, pipeline_mode=pl.Buffered(1))


def _mod_spec(layer, which):
    return pl.BlockSpec((None, None, MOD_ROWS, D_MODEL), lambda *_: (layer, which, 0, 0))


def _vec_spec(layer):
    return pl.BlockSpec((None, 1, D_MODEL), lambda *_: (layer, 0, 0))


def _params():
    return pltpu.CompilerParams(
        dimension_semantics=("arbitrary", "arbitrary"), vmem_limit_bytes=VMEM_LIMIT_BYTES)


def _mod_kernel(c_ref, w_ref, b_ref, o_ref):
    a_hi, a_lo = _split_bf16(_silu(c_ref[...]))
    w_hi, w_lo = _split_bf16(w_ref[...])
    o_ref[...] = _dot(a_hi, w_hi) + _dot(a_hi, w_lo) + _dot(a_lo, w_hi) + b_ref[...]


def _mod_call(cond, w_mod, b_mod):
    return pl.pallas_call(
        _mod_kernel,
        grid=(DEPTH, N_MOD),
        in_specs=[
            pl.BlockSpec((MOD_ROWS, D_MODEL), lambda l, j: (0, 0)),
            pl.BlockSpec((None, D_MODEL, D_MODEL), lambda l, j: (l, 0, j)),
            pl.BlockSpec((None, None, 1, D_MODEL), lambda l, j: (l, j, 0, 0)),
        ],
        out_specs=pl.BlockSpec((None, None, MOD_ROWS, D_MODEL), lambda l, j: (l, j, 0, 0)),
        out_shape=jax.ShapeDtypeStruct((DEPTH, N_MOD, MOD_ROWS, D_MODEL), F32),
        compiler_params=_params(),
        name="mod_vectors",
    )(cond, w_mod, b_mod.reshape(DEPTH, N_MOD, 1, D_MODEL))


QK_DIM = ATTN_DIM + KV_DIM
VU_DIM = KV_DIM + GM_DIM


def _rope(slab, cos, sin, first_of_pair):
    up = pltpu.roll(slab, LANES - HEAD_DIM // 4, 1)
    dn = pltpu.roll(slab, HEAD_DIM // 4, 1)
    return slab * cos + jnp.where(first_of_pair, up, dn) * sin


def _inproj_even_kernel(x_ref, sh_ref, sc_ref, g_ref, w_ref, cos_ref, sin_ref, p_ref, vnorm_ref,
                        ws_ref, bs_ref, q_ref, k_ref, v_ref, ogm_ref):
    b = pl.program_id(0)
    shift = sh_ref[pl.ds(b, 1), :]
    scale = sc_ref[pl.ds(b, 1), :]
    h = _modulate(x_ref[...], g_ref[...], shift, scale).astype(BF16)
    tm = h.shape[0]

    qk = _dot(h, w_ref[:, :QK_DIM])
    cos = cos_ref[...]
    sin = sin_ref[...]
    lane = lax.broadcasted_iota(jnp.int32, (tm, LANES), 1)
    first_of_pair = (lane & (HEAD_DIM // 4)) == 0
    for j in range(QK_DIM // LANES):
        roped = _rope(qk[:, j * LANES:(j + 1) * LANES], cos, sin, first_of_pair).astype(BF16)
        if j < ATTN_DIM // LANES:
            q_ref[:, j * LANES:(j + 1) * LANES] = roped
        else:
            k_ref[...] = roped

    vu = _dot(h, w_ref[:, QK_DIM:QK_DIM + VU_DIM])
    v_ref[...] = vu[:, :KV_DIM].astype(BF16)
    u = vu[:, KV_DIM:]

    d = _dot(h, w_ref[:, QK_DIM + VU_DIM:])
    d2_hi, d2_lo = _split_bf16(d * d)
    var = _dot(d2_hi, p_ref[...]) + _dot(d2_lo, p_ref[...])
    vn = (d * lax.rsqrt(var + LN_EPS) * vnorm_ref[...]).astype(BF16)

    low_half = lax.broadcasted_iota(jnp.int32, (CHUNK, LANES), 1) < HEAD_DIM
    zero = jnp.zeros((CHUNK, LANES), BF16)
    for c in range(tm // CHUNK):
        rows = slice(c * CHUNK, (c + 1) * CHUNK)
        for gp in range(GM_GROUPS // 2):
            cols = slice(gp * LANES, (gp + 1) * LANES)
            blk = vn[rows, cols]
            rhs = jnp.concatenate(
                [jnp.where(low_half, blk, zero), jnp.where(low_half, zero, blk)], axis=0)
            s = _dot(ws_ref[gp], rhs) + bs_ref[gp]
            ogm_ref[rows, cols] = (u[rows, cols] * s).astype(BF16)


def _inproj_even_call(x, mod, g_pre, w_in, cos_t, sin_t, p_mat, v_norm, ws2, bs2, layer):
    B, S, _ = x.shape
    tm = TOKENS_PER_STEP
    tok = lambda width: pl.BlockSpec((None, tm, width), lambda b, i: (b, i, 0))
    return pl.pallas_call(
        _inproj_even_kernel,
        grid=(B, S // tm),
        in_specs=[
            tok(D_MODEL),
            _mod_spec(layer, 0), _mod_spec(layer, 1), _vec_spec(layer),
            _resident((D_MODEL, IN_DIM_EVEN)),
            pl.BlockSpec((tm, LANES), lambda b, i: (i, 0)),
            pl.BlockSpec((tm, LANES), lambda b, i: (i, 0)),
            _resident((GM_DIM, GM_DIM)),
            _resident((1, GM_DIM)),
            _resident((GM_GROUPS // 2, CHUNK, 2 * CHUNK)),
            _resident((GM_GROUPS // 2, CHUNK, LANES)),
        ],
        out_specs=[tok(ATTN_DIM), tok(KV_DIM), tok(KV_DIM), tok(GM_DIM)],
        out_shape=[
            jax.ShapeDtypeStruct((B, S, ATTN_DIM), BF16),
            jax.ShapeDtypeStruct((B, S, KV_DIM), BF16),
            jax.ShapeDtypeStruct((B, S, KV_DIM), BF16),
            jax.ShapeDtypeStruct((B, S, GM_DIM), BF16),
        ],
        compiler_params=_params(),
        name="inproj_even",
    )(x, mod, mod, g_pre, w_in, cos_t, sin_t, p_mat, v_norm, ws2, bs2)


def _ctx_kv_kernel(ctx_ref, sh_ref, sc_ref, g_ref, w_ref, kc_ref, vc_ref, *, ctx_row):
    shift = sh_ref[ctx_row:ctx_row + 1, :]
    scale = sc_ref[ctx_row:ctx_row + 1, :]
    hc = _modulate(ctx_ref[...], g_ref[...], shift, scale).astype(BF16)
    kv = _dot(hc, w_ref[...])
    kc_ref[...] = kv[:, :KV_DIM].astype(BF16)
    vc_ref[...] = kv[:, KV_DIM:].astype(BF16)


def _ctx_kv_call(ctx, mod, g_pre, w_kv, layer, ctx_row):
    B, L, _ = ctx.shape
    out = pl.BlockSpec((None, L, KV_DIM), lambda b, i: (b, 0, 0))
    return pl.pallas_call(
        functools.partial(_ctx_kv_kernel, ctx_row=ctx_row),
        grid=(B, 1),
        in_specs=[
            pl.BlockSpec((None, L, D_MODEL), lambda b, i: (b, 0, 0)),
            _mod_spec(layer, 0), _mod_spec(layer, 1), _vec_spec(layer),
            _resident((D_MODEL, 2 * KV_DIM)),
        ],
        out_specs=[out, out],
        out_shape=[jax.ShapeDtypeStruct((B, L, KV_DIM), BF16)] * 2,
        compiler_params=_params(),
        name="ctx_kv",
    )(ctx, mod, mod, g_pre, w_kv)


def _attn_out_kernel(sink_ref, q_ref, kp_ref, km_ref, kn_ref, vp_ref, vm_ref, vn_ref, kc_ref, vc_ref,
                     ogm_ref, x_ref, gt_ref, gpost_ref, wo_ref, o_ref, oattn_ref):
    b = pl.program_id(0)
    i = pl.program_id(1)
    n_steps = pl.num_programs(1)
    tq = q_ref.shape[0]
    n_blocks = tq // BLOCK
    win = 3 * BLOCK

    kwin = jnp.concatenate([kp_ref[...], km_ref[...], kn_ref[...]], axis=0)
    vwin = jnp.concatenate([vp_ref[...], vm_ref[...], vn_ref[...]], axis=0)
    kc = kc_ref[...]
    vc = vc_ref[...]

    qi = lax.broadcasted_iota(jnp.int32, (BLOCK, win), 0)
    kj = lax.broadcasted_iota(jnp.int32, (BLOCK, win), 1)
    band = jnp.where(jnp.abs(kj - BLOCK - qi) <= WINDOW, 0.0, NEG_INF).astype(F32)
    before_start = jnp.where(kj < BLOCK, jnp.where(i == 0, NEG_INF, 0.0), 0.0).astype(F32)
    after_end = jnp.where(kj >= 2 * BLOCK, jnp.where(i == n_steps - 1, NEG_INF, 0.0), 0.0).astype(F32)

    for qb in range(n_blocks):
        rows = slice(qb * BLOCK, (qb + 1) * BLOCK)
        bias = band
        if qb == 0:
            bias = bias + before_start
        if qb == n_blocks - 1:
            bias = bias + after_end
        bias = jnp.concatenate([bias] * GQA, axis=0)
        q_blk = q_ref[rows, :]
        for j in range(KV_HEADS):
            heads = range(j * GQA, (j + 1) * GQA)
            qs = jnp.concatenate(
                [q_blk[:, h * HEAD_DIM:(h + 1) * HEAD_DIM] for h in heads], axis=0)
            kv_cols = slice(j * HEAD_DIM, (j + 1) * HEAD_DIM)
            kj_win = kwin[qb * BLOCK:qb * BLOCK + win, kv_cols]
            vj_win = vwin[qb * BLOCK:qb * BLOCK + win, kv_cols]
            s_win = _dot_nt(qs, kj_win) + bias
            s_ctx = _dot_nt(qs, kc[:, kv_cols])
            s_sink = jnp.concatenate(
                [jnp.full((BLOCK, 1), sink_ref[h], F32) for h in heads], axis=0)
            m = jnp.maximum(
                jnp.maximum(jnp.max(s_win, axis=-1, keepdims=True),
                            jnp.max(s_ctx, axis=-1, keepdims=True)), s_sink)
            p_win = jnp.exp(s_win - m)
            p_ctx = jnp.exp(s_ctx - m)
            denom = (jnp.sum(p_win, axis=-1, keepdims=True)
                     + jnp.sum(p_ctx, axis=-1, keepdims=True) + jnp.exp(s_sink - m))
            o = (_dot(p_win.astype(BF16), vj_win) + _dot(p_ctx.astype(BF16), vc[:, kv_cols])) / denom
            for g, h in enumerate(heads):
                oattn_ref[rows, h * HEAD_DIM:(h + 1) * HEAD_DIM] = (
                    o[g * BLOCK:(g + 1) * BLOCK, :].astype(BF16))

    y = _dot(oattn_ref[...], wo_ref[:ATTN_DIM, :]) + _dot(ogm_ref[...], wo_ref[ATTN_DIM:, :])
    gate = gt_ref[pl.ds(b, 1), :]
    o_ref[...] = x_ref[...] + gate * _rms_norm(y, gpost_ref[...])


def _attn_out_call(sink, q, k, v, kc, vc, ogm, x, mod, g_post, w_out, layer):
    B, S, _ = x.shape
    tq = TOKENS_PER_STEP
    bpt = tq // BLOCK
    n_blk = S // BLOCK
    L = kc.shape[1]
    tok = lambda width: pl.BlockSpec((None, tq, width), lambda b, i: (b, i, 0))
    prev = pl.BlockSpec((None, BLOCK, KV_DIM), lambda b, i: (b, jnp.maximum(i * bpt - 1, 0), 0))
    nxt = pl.BlockSpec((None, BLOCK, KV_DIM),
                       lambda b, i: (b, jnp.minimum((i + 1) * bpt, n_blk - 1), 0))
    ctx = pl.BlockSpec((None, L, KV_DIM), lambda b, i: (b, 0, 0))
    return pl.pallas_call(
        _attn_out_kernel,
        grid=(B, S // tq),
        in_specs=[
            pl.BlockSpec(memory_space=pltpu.SMEM),
            tok(ATTN_DIM),
            prev, tok(KV_DIM), nxt,
            prev, tok(KV_DIM), nxt,
            ctx, ctx,
            tok(GM_DIM), tok(D_MODEL),
            _mod_spec(layer, 2), _vec_spec(layer),
            _resident((ATTN_DIM + GM_DIM, D_MODEL)),
        ],
        out_specs=tok(D_MODEL),
        out_shape=jax.ShapeDtypeStruct((B, S, D_MODEL), F32),
        scratch_shapes=[pltpu.VMEM((tq, ATTN_DIM), BF16)],
        compiler_params=_params(),
        name="attn_out",
    )(sink, q, k, k, k, v, v, v, kc, vc, ogm, x, mod, g_post, w_out)


def _ffn_kernel(x_ref, sh_ref, sc_ref, gt_ref, gpre_ref, gpost_ref, w1_ref, w3_ref, w2_ref, o_ref,
                acc_ref):
    b = pl.program_id(0)
    x = x_ref[...]
    hf = _modulate(x, gpre_ref[...], sh_ref[pl.ds(b, 1), :], sc_ref[pl.ds(b, 1), :]).astype(BF16)
    for c in range(D_FF // FF_CHUNK):
        cols = slice(c * FF_CHUNK, (c + 1) * FF_CHUNK)
        gated = (_silu(_dot(hf, w1_ref[:, cols])) * _dot(hf, w3_ref[:, cols])).astype(BF16)
        part = _dot(gated, w2_ref[cols, :])
        if c == 0:
            acc_ref[...] = part
        else:
            acc_ref[...] += part
    o_ref[...] = x + gt_ref[pl.ds(b, 1), :] * _rms_norm(acc_ref[...], gpost_ref[...])


def _ffn_call(x, mod, g_pre, g_post, w1, w3, w2, layer):
    B, S, _ = x.shape
    tm = TOKENS_PER_STEP
    tok = pl.BlockSpec((None, tm, D_MODEL), lambda b, i: (b, i, 0))
    return pl.pallas_call(
        _ffn_kernel,
        grid=(B, S // tm),
        in_specs=[
            tok,
            _mod_spec(layer, 3), _mod_spec(layer, 4), _mod_spec(layer, 5),
            _vec_spec(layer), _vec_spec(layer),
            pl.BlockSpec((None, D_MODEL, D_FF), lambda b, i: (layer, 0, 0), pipeline_mode=pl.Buffered(1)),
            pl.BlockSpec((None, D_MODEL, D_FF), lambda b, i: (layer, 0, 0), pipeline_mode=pl.Buffered(1)),
            pl.BlockSpec((None, D_FF, D_MODEL), lambda b, i: (layer, 0, 0), pipeline_mode=pl.Buffered(1)),
        ],
        out_specs=tok,
        out_shape=jax.ShapeDtypeStruct((B, S, D_MODEL), F32),
        scratch_shapes=[pltpu.VMEM((tm, D_MODEL), F32)],
        compiler_params=_params(),
        name="swiglu_ffn",
    )(x, mod, mod, mod, g_pre, g_post, w1, w3, w2)


HALO = SUBLANES_F32


def _conv_mixer_kernel(xp_ref, x_ref, xn_ref, sh_ref, sc_ref, gt_ref, gpre_ref, gpost_ref, wi_ref,
                       cw_ref, wo_ref, o_ref):
    b = pl.program_id(0)
    i = pl.program_id(1)
    n_steps = pl.num_programs(1)
    x = x_ref[...]
    tm = x.shape[0]
    ext = jnp.concatenate([xp_ref[...], x, xn_ref[...]], axis=0)
    h = _modulate(ext, gpre_ref[...], sh_ref[pl.ds(b, 1), :], sc_ref[pl.ds(b, 1), :]).astype(BF16)
    y = _dot(h, wi_ref[:, D_MODEL:2 * D_MODEL]) * _dot(h, wi_ref[:, 2 * D_MODEL:])
    r = lax.broadcasted_iota(jnp.int32, (tm + 2 * HALO, 1), 0)
    first_row = jnp.where(i > 0, 0, HALO)
    end_row = jnp.where(i < n_steps - 1, tm + 2 * HALO, tm + HALO)
    inside = (r >= first_row) & (r < end_row)
    y = jnp.where(inside, y, 0.0)
    cw = cw_ref[...]
    yconv = (cw[0:1, :] * y[HALO - 1:HALO - 1 + tm, :] + cw[1:2, :] * y[HALO:HALO + tm, :]
             + cw[2:3, :] * y[HALO + 1:HALO + 1 + tm, :])
    bg = _dot(h[HALO:HALO + tm, :], wi_ref[:, :D_MODEL])
    out = _dot((bg * yconv).astype(BF16), wo_ref[...])
    o_ref[...] = x + gt_ref[pl.ds(b, 1), :] * _rms_norm(out, gpost_ref[...])


def _conv_mixer_call(x, mod, g_pre, g_post, w_in, conv_w, w_out, layer):
    B, S, _ = x.shape
    tm = TOKENS_PER_STEP
    hpt = tm // HALO
    n_halo = S // HALO
    tok = pl.BlockSpec((None, tm, D_MODEL), lambda b, i: (b, i, 0))
    prev = pl.BlockSpec((None, HALO, D_MODEL), lambda b, i: (b, jnp.maximum(i * hpt - 1, 0), 0))
    nxt = pl.BlockSpec((None, HALO, D_MODEL),
                       lambda b, i: (b, jnp.minimum((i + 1) * hpt, n_halo - 1), 0))
    return pl.pallas_call(
        _conv_mixer_kernel,
        grid=(B, S // tm),
        in_specs=[
            prev, tok, nxt,
            _mod_spec(layer, 0), _mod_spec(layer, 1), _mod_spec(layer, 2),
            _vec_spec(layer), _vec_spec(layer),
            _resident((D_MODEL, 3 * D_MODEL)),
            _resident((SUBLANES_F32, D_MODEL)),
            _resident((D_MODEL, D_MODEL)),
        ],
        out_specs=tok,
        out_shape=jax.ShapeDtypeStruct((B, S, D_MODEL), F32),
        compiler_params=_params(),
        name="conv_mixer",
    )(x, x, x, mod, mod, mod, g_pre, g_post, w_in, conv_w, w_out)


def _rope_tables(seq):
    quarter = HEAD_DIM // 4
    rows = seq // GRID_W
    inv = ROPE_THETA ** (-jnp.arange(quarter, dtype=F32) / quarter)
    ra = jnp.arange(rows, dtype=F32)[:, None] * inv
    ca = jnp.arange(GRID_W, dtype=F32)[:, None] * inv
    full = (rows, GRID_W, quarter)
    r_cos = jnp.broadcast_to(jnp.cos(ra)[:, None, :], full)
    r_sin = jnp.broadcast_to(jnp.sin(ra)[:, None, :], full)
    c_cos = jnp.broadcast_to(jnp.cos(ca)[None, :, :], full)
    c_sin = jnp.broadcast_to(jnp.sin(ca)[None, :, :], full)
    cos_h = jnp.concatenate([r_cos, r_cos, c_cos, c_cos], axis=-1)
    sin_h = jnp.concatenate([-r_sin, r_sin, -c_sin, c_sin], axis=-1)
    reps = LANES // HEAD_DIM
    cos_t = jnp.tile(cos_h, (1, 1, reps)).reshape(seq, LANES)
    sin_t = jnp.tile(sin_h, (1, 1, reps)).reshape(seq, LANES)
    return cos_t, sin_t


def _prep_even_weights(w_in, ws, bs):
    wq = w_in[:, :ATTN_DIM] * (HEAD_DIM ** -0.5)
    rest = w_in[:, ATTN_DIM:ATTN_DIM + 2 * KV_DIM + GM_DIM]
    wg = w_in[:, IN_DIM_EVEN - GM_DIM:].reshape(D_MODEL, GM_GROUPS, HEAD_DIM)
    wg = (wg - jnp.mean(wg, axis=-1, keepdims=True)).reshape(D_MODEL, GM_DIM)
    w_all = jnp.concatenate([wq, rest, wg], axis=1).astype(BF16)
    w_kv = w_in[:, ATTN_DIM:ATTN_DIM + 2 * KV_DIM].astype(BF16)
    half = GM_GROUPS // 2
    ws2 = ws.reshape(half, 2, CHUNK, CHUNK).transpose(0, 2, 1, 3).reshape(half, CHUNK, 2 * CHUNK)
    bs2 = jnp.repeat(bs.reshape(half, 2, CHUNK).transpose(0, 2, 1), HEAD_DIM, axis=-1)
    return w_all, w_kv, ws2.astype(BF16), bs2.astype(F32)


def kernel(x, c, ctx, c_ctx, w_mod, b_mod, g_mix_pre, g_mix_post, g_ffn_pre, g_ffn_post, ffn_w1, ffn_w3,
           ffn_w2, a_w_in, a_sink, gm_v_norm, gm_ws, gm_bs, a_w_out, sc_w_in, sc_conv, sc_w_out):
    B, S, D = x.shape
    assert D == D_MODEL and S % TOKENS_PER_STEP == 0 and B + 1 <= MOD_ROWS
    assert a_w_in.shape[0] == 1 and sc_w_in.shape[0] == 1 and w_mod.shape[0] == DEPTH

    cond = jnp.concatenate([c, c_ctx[None, :], jnp.zeros((MOD_ROWS - B - 1, D), F32)], axis=0)
    mod = _mod_call(cond, w_mod, b_mod)

    vec = lambda g: g.reshape(DEPTH, 1, D)
    g_mix_pre, g_mix_post, g_ffn_pre, g_ffn_post = map(vec, (g_mix_pre, g_mix_post, g_ffn_pre, g_ffn_post))
    w1, w3, w2 = ffn_w1.astype(BF16), ffn_w3.astype(BF16), ffn_w2.astype(BF16)

    w_all, w_kv, ws2, bs2 = _prep_even_weights(a_w_in[0], gm_ws[0], gm_bs[0])
    cos_t, sin_t = _rope_tables(S)
    group = jnp.arange(GM_DIM) // HEAD_DIM
    p_mat = jnp.where(group[:, None] == group[None, :], 1.0 / HEAD_DIM, 0.0).astype(BF16)
    q, k, v, ogm = _inproj_even_call(x, mod, g_mix_pre, w_all, cos_t, sin_t, p_mat,
                                     gm_v_norm[0].reshape(1, GM_DIM), ws2, bs2, layer=0)
    kc, vc = _ctx_kv_call(ctx, mod, g_mix_pre, w_kv, layer=0, ctx_row=B)
    x = _attn_out_call(a_sink[0], q, k, v, kc, vc, ogm, x, mod, g_mix_post, a_w_out[0].astype(BF16),
                       layer=0)
    x = _ffn_call(x, mod, g_ffn_pre, g_ffn_post, w1, w3, w2, layer=0)

    conv_w = jnp.concatenate([sc_conv[0], jnp.zeros((SUBLANES_F32 - CONV_WIDTH, D), F32)], axis=0)
    x = _conv_mixer_call(x, mod, g_mix_pre, g_mix_post, sc_w_in[0].astype(BF16), conv_w,
                         sc_w_out[0].astype(BF16), layer=1)
    x = _ffn_call(x, mod, g_ffn_pre, g_ffn_post, w1, w3, w2, layer=1)
    return x
```

```python
import functools

import jax
import jax.numpy as jnp
from jax import lax
from jax.experimental import pallas as pl
from jax.experimental.pallas import tpu as pltpu

D_MODEL = 1024
DEPTH = 2
GRID_W = 64
HEAD_DIM = 64
Q_HEADS = 8
KV_HEADS = 2
GQA = Q_HEADS // KV_HEADS
WINDOW = 128
BLOCK = 128
ATTN_DIM = Q_HEADS * HEAD_DIM
KV_DIM = KV_HEADS * HEAD_DIM
GM_GROUPS = 8
GM_DIM = GM_GROUPS * HEAD_DIM
CHUNK = 128
IN_DIM_EVEN = ATTN_DIM + 2 * KV_DIM + 2 * GM_DIM
CONV_WIDTH = 3
D_FF = 2816
ROPE_THETA = 10000.0
RMS_EPS = 1e-6
LN_EPS = 1e-5
NEG_INF = -1e30
N_MOD = 6

LANES = 128
SUBLANES_F32 = 8
MXU_WIDTH = 256
VMEM_LIMIT_BYTES = 56 * 1024 * 1024

TOKENS_PER_STEP = 512
FF_CHUNK = MXU_WIDTH
MOD_ROWS = 8

F32 = jnp.float32
BF16 = jnp.bfloat16


def _dot(a, b):
    return jnp.dot(a, b, preferred_element_type=F32)


def _dot_nt(a, b):
    return lax.dot_general(a, b, (((1,), (1,)), ((), ())), preferred_element_type=F32)


def _rms_norm(x, g):
    ms = jnp.mean(x * x, axis=-1, keepdims=True)
    return x * lax.rsqrt(ms + RMS_EPS) * g


def _modulate(x, g, shift, scale):
    return _rms_norm(x, g) * (1.0 + scale) + shift


def _silu(x):
    return x * jax.nn.sigmoid(x)


def _split_bf16(x):
    hi = x.astype(BF16)
    lo = (x - hi.astype(F32)).astype(BF16)
    return hi, lo


def _resident(shape):
    nd = len(shape)
    return pl.BlockSpec(shape, lambda *_: (0,) * nd, pipeline_mode=pl.Buffered(1))


def _mod_spec(layer, which):
    return pl.BlockSpec((None, None, MOD_ROWS, D_MODEL), lambda *_: (layer, which, 0, 0))


def _vec_spec(layer):
    return pl.BlockSpec((None, 1, D_MODEL), lambda *_: (layer, 0, 0))


def _params():
    return pltpu.CompilerParams(
        dimension_semantics=("arbitrary", "arbitrary"), vmem_limit_bytes=VMEM_LIMIT_BYTES)


def _mod_kernel(c_ref, w_ref, b_ref, o_ref):
    a_hi, a_lo = _split_bf16(_silu(c_ref[...]))
    w_hi, w_lo = _split_bf16(w_ref[...])
    o_ref[...] = _dot(a_hi, w_hi) + _dot(a_hi, w_lo) + _dot(a_lo, w_hi) + b_ref[...]


def _mod_call(cond, w_mod, b_mod):
    return pl.pallas_call(
        _mod_kernel,
        grid=(DEPTH, N_MOD),
        in_specs=[
            pl.BlockSpec((MOD_ROWS, D_MODEL), lambda l, j: (0, 0)),
            pl.BlockSpec((None, D_MODEL, D_MODEL), lambda l, j: (l, 0, j)),
            pl.BlockSpec((None, None, 1, D_MODEL), lambda l, j: (l, j, 0, 0)),
        ],
        out_specs=pl.BlockSpec((None, None, MOD_ROWS, D_MODEL), lambda l, j: (l, j, 0, 0)),
        out_shape=jax.ShapeDtypeStruct((DEPTH, N_MOD, MOD_ROWS, D_MODEL), F32),
        compiler_params=_params(),
        name="mod_vectors",
    )(cond, w_mod, b_mod.reshape(DEPTH, N_MOD, 1, D_MODEL))


QK_DIM = ATTN_DIM + KV_DIM
VU_DIM = KV_DIM + GM_DIM


def _rope(slab, cos, sin, first_of_pair):
    up = pltpu.roll(slab, LANES - HEAD_DIM // 4, 1)
    dn = pltpu.roll(slab, HEAD_DIM // 4, 1)
    return slab * cos + jnp.where(first_of_pair, up, dn) * sin


def _token_table(row_ref, col_ref):
    col = col_ref[...]
    return jnp.concatenate(
        [row_ref[r:r + 1, :] + col for r in range(row_ref.shape[0])], axis=0)


def _inproj_even_kernel(x_ref, sh_ref, sc_ref, g_ref, w_ref, rcos_ref, rsin_ref, ccos_ref, csin_ref,
                        p_ref, vnorm_ref, ws_ref, bs_ref, q_ref, k_ref, v_ref, ogm_ref):
    b = pl.program_id(0)
    shift = sh_ref[pl.ds(b, 1), :]
    scale = sc_ref[pl.ds(b, 1), :]
    h = _modulate(x_ref[...], g_ref[...], shift, scale).astype(BF16)
    tm = h.shape[0]

    qk = _dot(h, w_ref[:, :QK_DIM])
    cos = _token_table(rcos_ref, ccos_ref)
    sin = _token_table(rsin_ref, csin_ref)
    lane = lax.broadcasted_iota(jnp.int32, (tm, LANES), 1)
    first_of_pair = (lane & (HEAD_DIM // 4)) == 0
    for j in range(QK_DIM // LANES):
        roped = _rope(qk[:, j * LANES:(j + 1) * LANES], cos, sin, first_of_pair).astype(BF16)
        if j < ATTN_DIM // LANES:
            q_ref[:, j * LANES:(j + 1) * LANES] = roped
        else:
            k_ref[...] = roped

    vu = _dot(h, w_ref[:, QK_DIM:QK_DIM + VU_DIM])
    v_ref[...] = vu[:, :KV_DIM].astype(BF16)
    u = vu[:, KV_DIM:]

    d = _dot(h, w_ref[:, QK_DIM + VU_DIM:])
    d2_hi, d2_lo = _split_bf16(d * d)
    var = _dot(d2_hi, p_ref[...]) + _dot(d2_lo, p_ref[...])
    vn = (d * lax.rsqrt(var + LN_EPS) * vnorm_ref[...]).astype(BF16)

    low_half = lax.broadcasted_iota(jnp.int32, (CHUNK, LANES), 1) < HEAD_DIM
    zero = jnp.zeros((CHUNK, LANES), BF16)
    for c in range(tm // CHUNK):
        rows = slice(c * CHUNK, (c + 1) * CHUNK)
        for gp in range(GM_GROUPS // 2):
            cols = slice(gp * LANES, (gp + 1) * LANES)
            blk = vn[rows, cols]
            rhs = jnp.concatenate(
                [jnp.where(low_half, blk, zero), jnp.where(low_half, zero, blk)], axis=0)
            s = _dot(ws_ref[gp], rhs) + bs_ref[gp]
            ogm_ref[rows, cols] = (u[rows, cols] * s).astype(BF16)


def _inproj_even_call(x, mod, g_pre, w_in, rope, p_mat, v_norm, ws2, bs2, layer):
    B, S, _ = x.shape
    tm = TOKENS_PER_STEP
    assert tm // GRID_W == SUBLANES_F32
    row_cos, row_sin, col_cos, col_sin = rope
    tok = lambda width: pl.BlockSpec((None, tm, width), lambda b, i: (b, i, 0))
    row_tab = pl.BlockSpec((tm // GRID_W, LANES), lambda b, i: (i, 0))
    return pl.pallas_call(
        _inproj_even_kernel,
        grid=(B, S // tm),
        in_specs=[
            tok(D_MODEL),
            _mod_spec(layer, 0), _mod_spec(layer, 1), _vec_spec(layer),
            _resident((D_MODEL, IN_DIM_EVEN)),
            row_tab, row_tab, _resident((GRID_W, LANES)), _resident((GRID_W, LANES)),
            _resident((GM_DIM, GM_DIM)),
            _resident((1, GM_DIM)),
            _resident((GM_GROUPS // 2, CHUNK, 2 * CHUNK)),
            _resident((GM_GROUPS // 2, CHUNK, LANES)),
        ],
        out_specs=[tok(ATTN_DIM), tok(KV_DIM), tok(KV_DIM), tok(GM_DIM)],
        out_shape=[
            jax.ShapeDtypeStruct((B, S, ATTN_DIM), BF16),
            jax.ShapeDtypeStruct((B, S, KV_DIM), BF16),
            jax.ShapeDtypeStruct((B, S, KV_DIM), BF16),
            jax.ShapeDtypeStruct((B, S, GM_DIM), BF16),
        ],
        compiler_params=_params(),
        name="inproj_even",
    )(x, mod, mod, g_pre, w_in, row_cos, row_sin, col_cos, col_sin, p_mat, v_norm, ws2, bs2)


def _ctx_kv_kernel(ctx_ref, sh_ref, sc_ref, g_ref, w_ref, kc_ref, vc_ref, *, ctx_row):
    shift = sh_ref[ctx_row:ctx_row + 1, :]
    scale = sc_ref[ctx_row:ctx_row + 1, :]
    hc = _modulate(ctx_ref[...], g_ref[...], shift, scale).astype(BF16)
    kv = _dot(hc, w_ref[...])
    kc_ref[...] = kv[:, :KV_DIM].astype(BF16)
    vc_ref[...] = kv[:, KV_DIM:].astype(BF16)


def _ctx_kv_call(ctx, mod, g_pre, w_kv, layer, ctx_row):
    B, L, _ = ctx.shape
    out = pl.BlockSpec((None, L, KV_DIM), lambda b, i: (b, 0, 0))
    return pl.pallas_call(
        functools.partial(_ctx_kv_kernel, ctx_row=ctx_row),
        grid=(B, 1),
        in_specs=[
            pl.BlockSpec((None, L, D_MODEL), lambda b, i: (b, 0, 0)),
            _mod_spec(layer, 0), _mod_spec(layer, 1), _vec_spec(layer),
            _resident((D_MODEL, 2 * KV_DIM)),
        ],
        out_specs=[out, out],
        out_shape=[jax.ShapeDtypeStruct((B, L, KV_DIM), BF16)] * 2,
        compiler_params=_params(),
        name="ctx_kv",
    )(ctx, mod, mod, g_pre, w_kv)


def _attn_out_kernel(sink_ref, q_ref, kp_ref, km_ref, kn_ref, vp_ref, vm_ref, vn_ref, kc_ref, vc_ref,
                     ogm_ref, x_ref, gt_ref, gpost_ref, wo_ref, o_ref, oattn_ref):
    b = pl.program_id(0)
    i = pl.program_id(1)
    n_steps = pl.num_programs(1)
    tq = q_ref.shape[0]
    n_blocks = tq // BLOCK
    win = 3 * BLOCK

    kwin = jnp.concatenate([kp_ref[...], km_ref[...], kn_ref[...]], axis=0)
    vwin = jnp.concatenate([vp_ref[...], vm_ref[...], vn_ref[...]], axis=0)
    kc = kc_ref[...]
    vc = vc_ref[...]
    n_keys = win + kc.shape[0]

    qi = lax.broadcasted_iota(jnp.int32, (BLOCK, win), 0)
    kj = lax.broadcasted_iota(jnp.int32, (BLOCK, win), 1)
    band = jnp.where(jnp.abs(kj - BLOCK - qi) <= WINDOW, 0.0, NEG_INF).astype(F32)
    before_start = jnp.where(kj < BLOCK, jnp.where(i == 0, NEG_INF, 0.0), 0.0).astype(F32)
    after_end = jnp.where(kj >= 2 * BLOCK, jnp.where(i == n_steps - 1, NEG_INF, 0.0), 0.0).astype(F32)

    q_half = [(lax.broadcasted_iota(jnp.int32, (BLOCK, LANES), 1) // HEAD_DIM) == j
              for j in range(KV_HEADS)]
    kv_half = [(lax.broadcasted_iota(jnp.int32, (n_keys, LANES), 1) // HEAD_DIM) == j
               for j in range(KV_HEADS)]

    for qb in range(n_blocks):
        rows = slice(qb * BLOCK, (qb + 1) * BLOCK)
        bias = band
        if qb == 0:
            bias = bias + before_start
        if qb == n_blocks - 1:
            bias = bias + after_end
        q_blk = q_ref[rows, :]
        k_all = jnp.concatenate([kwin[qb * BLOCK:qb * BLOCK + win, :], kc], axis=0)
        v_all = jnp.concatenate([vwin[qb * BLOCK:qb * BLOCK + win, :], vc], axis=0)
        normed = []
        for j in range(KV_HEADS):
            qs = jnp.concatenate(
                [jnp.where(q_half[j], q_blk[:, g * LANES:(g + 1) * LANES], jnp.zeros((), BF16))
                 for g in range(GQA)], axis=0)
            s = _dot_nt(qs, k_all)
            probs, sink_terms = [], []
            for g in range(GQA):
                sink = sink_ref[j * GQA + g]
                sg = s[g * BLOCK:(g + 1) * BLOCK, :]
                sg = jnp.concatenate([sg[:, :win] + bias, sg[:, win:]], axis=1)
                m = jnp.maximum(jnp.max(sg, axis=-1, keepdims=True), sink)
                probs.append(jnp.exp(sg - m).astype(BF16))
                sink_terms.append(jnp.exp(sink - m))
            v_ext = jnp.where(kv_half[j], v_all, jnp.ones((), BF16))
            e = _dot(jnp.concatenate(probs, axis=0), v_ext)
            outs = []
            for g in range(GQA):
                eg = e[g * BLOCK:(g + 1) * BLOCK, :]
                inv = 1.0 / (eg + sink_terms[g])
                outs.append(eg * pltpu.roll(inv, HEAD_DIM, 1))
            normed.append(outs)
        for g in range(GQA):
            slab = jnp.where(q_half[0], normed[0][g], normed[1][g])
            oattn_ref[rows, g * LANES:(g + 1) * LANES] = slab.astype(BF16)

    y = _dot(oattn_ref[...], wo_ref[:ATTN_DIM, :]) + _dot(ogm_ref[...], wo_ref[ATTN_DIM:, :])
    gate = gt_ref[pl.ds(b, 1), :]
    o_ref[...] = x_ref[...] + gate * _rms_norm(y, gpost_ref[...])


def _attn_out_call(sink, q, k, v, kc, vc, ogm, x, mod, g_post, w_out, layer):
    B, S, _ = x.shape
    tq = TOKENS_PER_STEP
    bpt = tq // BLOCK
    n_blk = S // BLOCK
    L = kc.shape[1]
    tok = lambda width: pl.BlockSpec((None, tq, width), lambda b, i: (b, i, 0))
    prev = pl.BlockSpec((None, BLOCK, KV_DIM), lambda b, i: (b, jnp.maximum(i * bpt - 1, 0), 0))
    nxt = pl.BlockSpec((None, BLOCK, KV_DIM),
                       lambda b, i: (b, jnp.minimum((i + 1) * bpt, n_blk - 1), 0))
    ctx = pl.BlockSpec((None, L, KV_DIM), lambda b, i: (b, 0, 0))
    return pl.pallas_call(
        _attn_out_kernel,
        grid=(B, S // tq),
        in_specs=[
            pl.BlockSpec(memory_space=pltpu.SMEM),
            tok(ATTN_DIM),
            prev, tok(KV_DIM), nxt,
            prev, tok(KV_DIM), nxt,
            ctx, ctx,
            tok(GM_DIM), tok(D_MODEL),
            _mod_spec(layer, 2), _vec_spec(layer),
            _resident((ATTN_DIM + GM_DIM, D_MODEL)),
        ],
        out_specs=tok(D_MODEL),
        out_shape=jax.ShapeDtypeStruct((B, S, D_MODEL), F32),
        scratch_shapes=[pltpu.VMEM((tq, ATTN_DIM), BF16)],
        compiler_params=_params(),
        name="attn_out",
    )(sink, q, k, k, k, v, v, v, kc, vc, ogm, x, mod, g_post, w_out)


def _ffn_kernel(x_ref, sh_ref, sc_ref, gt_ref, gpre_ref, gpost_ref, w1_ref, w3_ref, w2_ref, o_ref,
                acc_ref):
    b = pl.program_id(0)
    x = x_ref[...]
    hf = _modulate(x, gpre_ref[...], sh_ref[pl.ds(b, 1), :], sc_ref[pl.ds(b, 1), :]).astype(BF16)
    for c in range(D_FF // FF_CHUNK):
        cols = slice(c * FF_CHUNK, (c + 1) * FF_CHUNK)
        gated = (_silu(_dot(hf, w1_ref[:, cols])) * _dot(hf, w3_ref[:, cols])).astype(BF16)
        part = _dot(gated, w2_ref[cols, :])
        if c == 0:
            acc_ref[...] = part
        else:
            acc_ref[...] += part
    o_ref[...] = x + gt_ref[pl.ds(b, 1), :] * _rms_norm(acc_ref[...], gpost_ref[...])


def _ffn_call(x, mod, g_pre, g_post, w1, w3, w2, layer):
    B, S, _ = x.shape
    tm = TOKENS_PER_STEP
    tok = pl.BlockSpec((None, tm, D_MODEL), lambda b, i: (b, i, 0))
    return pl.pallas_call(
        _ffn_kernel,
        grid=(B, S // tm),
        in_specs=[
            tok,
            _mod_spec(layer, 3), _mod_spec(layer, 4), _mod_spec(layer, 5),
            _vec_spec(layer), _vec_spec(layer),
            pl.BlockSpec((None, D_MODEL, D_FF), lambda b, i: (layer, 0, 0), pipeline_mode=pl.Buffered(1)),
            pl.BlockSpec((None, D_MODEL, D_FF), lambda b, i: (layer, 0, 0), pipeline_mode=pl.Buffered(1)),
            pl.BlockSpec((None, D_FF, D_MODEL), lambda b, i: (layer, 0, 0), pipeline_mode=pl.Buffered(1)),
        ],
        out_specs=tok,
        out_shape=jax.ShapeDtypeStruct((B, S, D_MODEL), F32),
        scratch_shapes=[pltpu.VMEM((tm, D_MODEL), F32)],
        compiler_params=_params(),
        name="swiglu_ffn",
    )(x, mod, mod, mod, g_pre, g_post, w1, w3, w2)


HALO = SUBLANES_F32


def _conv_mixer_kernel(xp_ref, x_ref, xn_ref, sh_ref, sc_ref, gt_ref, gpre_ref, gpost_ref, wi_ref,
                       cw_ref, wo_ref, o_ref):
    b = pl.program_id(0)
    i = pl.program_id(1)
    n_steps = pl.num_programs(1)
    x = x_ref[...]
    tm = x.shape[0]
    ext = jnp.concatenate([xp_ref[...], x, xn_ref[...]], axis=0)
    h = _modulate(ext, gpre_ref[...], sh_ref[pl.ds(b, 1), :], sc_ref[pl.ds(b, 1), :]).astype(BF16)
    y = _dot(h, wi_ref[:, D_MODEL:2 * D_MODEL]) * _dot(h, wi_ref[:, 2 * D_MODEL:])
    r = lax.broadcasted_iota(jnp.int32, (tm + 2 * HALO, 1), 0)
    first_row = jnp.where(i > 0, 0, HALO)
    end_row = jnp.where(i < n_steps - 1, tm + 2 * HALO, tm + HALO)
    inside = (r >= first_row) & (r < end_row)
    y = jnp.where(inside, y, 0.0)
    cw = cw_ref[...]
    yconv = (cw[0:1, :] * y[HALO - 1:HALO - 1 + tm, :] + cw[1:2, :] * y[HALO:HALO + tm, :]
             + cw[2:3, :] * y[HALO + 1:HALO + 1 + tm, :])
    bg = _dot(h[HALO:HALO + tm, :], wi_ref[:, :D_MODEL])
    out = _dot((bg * yconv).astype(BF16), wo_ref[...])
    o_ref[...] = x + gt_ref[pl.ds(b, 1), :] * _rms_norm(out, gpost_ref[...])


def _conv_mixer_call(x, mod, g_pre, g_post, w_in, conv_w, w_out, layer):
    B, S, _ = x.shape
    tm = TOKENS_PER_STEP
    hpt = tm // HALO
    n_halo = S // HALO
    tok = pl.BlockSpec((None, tm, D_MODEL), lambda b, i: (b, i, 0))
    prev = pl.BlockSpec((None, HALO, D_MODEL), lambda b, i: (b, jnp.maximum(i * hpt - 1, 0), 0))
    nxt = pl.BlockSpec((None, HALO, D_MODEL),
                       lambda b, i: (b, jnp.minimum((i + 1) * hpt, n_halo - 1), 0))
    return pl.pallas_call(
        _conv_mixer_kernel,
        grid=(B, S // tm),
        in_specs=[
            prev, tok, nxt,
            _mod_spec(layer, 0), _mod_spec(layer, 1), _mod_spec(layer, 2),
            _vec_spec(layer), _vec_spec(layer),
            _resident((D_MODEL, 3 * D_MODEL)),
            _resident((SUBLANES_F32, D_MODEL)),
            _resident((D_MODEL, D_MODEL)),
        ],
        out_specs=tok,
        out_shape=jax.ShapeDtypeStruct((B, S, D_MODEL), F32),
        compiler_params=_params(),
        name="conv_mixer",
    )(x, x, x, mod, mod, mod, g_pre, g_post, w_in, conv_w, w_out)


def _rope_tables(seq):
    quarter = HEAD_DIM // 4
    rows = seq // GRID_W
    inv = ROPE_THETA ** (-jnp.arange(quarter, dtype=F32) / quarter)
    ra = jnp.arange(rows, dtype=F32)[:, None] * inv
    ca = jnp.arange(GRID_W, dtype=F32)[:, None] * inv
    reps = LANES // HEAD_DIM
    zr, zc = jnp.zeros_like(ra), jnp.zeros_like(ca)
    lanes = lambda parts: jnp.tile(jnp.concatenate(parts, axis=-1), (1, reps))
    row_cos = lanes([jnp.cos(ra), jnp.cos(ra), zr, zr])
    row_sin = lanes([-jnp.sin(ra), jnp.sin(ra), zr, zr])
    col_cos = lanes([zc, zc, jnp.cos(ca), jnp.cos(ca)])
    col_sin = lanes([zc, zc, -jnp.sin(ca), jnp.sin(ca)])
    return row_cos, row_sin, col_cos, col_sin


def _to_head_slabs(w, axis):
    shape = w.shape
    split = shape[:axis] + (KV_HEADS, GQA, HEAD_DIM) + shape[axis + 1:]
    return jnp.swapaxes(w.reshape(split), axis, axis + 1).reshape(shape)


def _prep_even_weights(w_in, ws, bs):
    wq = _to_head_slabs(w_in[:, :ATTN_DIM], 1) * (HEAD_DIM ** -0.5)
    rest = w_in[:, ATTN_DIM:ATTN_DIM + 2 * KV_DIM + GM_DIM]
    wg = w_in[:, IN_DIM_EVEN - GM_DIM:].reshape(D_MODEL, GM_GROUPS, HEAD_DIM)
    wg = (wg - jnp.mean(wg, axis=-1, keepdims=True)).reshape(D_MODEL, GM_DIM)
    w_all = jnp.concatenate([wq, rest, wg], axis=1).astype(BF16)
    w_kv = w_in[:, ATTN_DIM:ATTN_DIM + 2 * KV_DIM].astype(BF16)
    half = GM_GROUPS // 2
    ws2 = ws.reshape(half, 2, CHUNK, CHUNK).transpose(0, 2, 1, 3).reshape(half, CHUNK, 2 * CHUNK)
    bs2 = jnp.repeat(bs.reshape(half, 2, CHUNK).transpose(0, 2, 1), HEAD_DIM, axis=-1)
    return w_all, w_kv, ws2.astype(BF16), bs2.astype(F32)


def kernel(x, c, ctx, c_ctx, w_mod, b_mod, g_mix_pre, g_mix_post, g_ffn_pre, g_ffn_post, ffn_w1, ffn_w3,
           ffn_w2, a_w_in, a_sink, gm_v_norm, gm_ws, gm_bs, a_w_out, sc_w_in, sc_conv, sc_w_out):
    B, S, D = x.shape
    assert D == D_MODEL and S % TOKENS_PER_STEP == 0 and B + 1 <= MOD_ROWS
    assert a_w_in.shape[0] == 1 and sc_w_in.shape[0] == 1 and w_mod.shape[0] == DEPTH

    cond = jnp.concatenate([c, c_ctx[None, :], jnp.zeros((MOD_ROWS - B - 1, D), F32)], axis=0)
    mod = _mod_call(cond, w_mod, b_mod)

    vec = lambda g: g.reshape(DEPTH, 1, D)
    g_mix_pre, g_mix_post, g_ffn_pre, g_ffn_post = map(vec, (g_mix_pre, g_mix_post, g_ffn_pre, g_ffn_post))
    w1, w3, w2 = ffn_w1.astype(BF16), ffn_w3.astype(BF16), ffn_w2.astype(BF16)

    w_all, w_kv, ws2, bs2 = _prep_even_weights(a_w_in[0], gm_ws[0], gm_bs[0])
    group = jnp.arange(GM_DIM) // HEAD_DIM
    p_mat = jnp.where(group[:, None] == group[None, :], 1.0 / HEAD_DIM, 0.0).astype(BF16)
    q, k, v, ogm = _inproj_even_call(x, mod, g_mix_pre, w_all, _rope_tables(S), p_mat,
                                     gm_v_norm[0].reshape(1, GM_DIM), ws2, bs2, layer=0)
    kc, vc = _ctx_kv_call(ctx, mod, g_mix_pre, w_kv, layer=0, ctx_row=B)
    w_out = jnp.concatenate([_to_head_slabs(a_w_out[0][:ATTN_DIM], 0), a_w_out[0][ATTN_DIM:]], axis=0)
    x = _attn_out_call(a_sink[0], q, k, v, kc, vc, ogm, x, mod, g_mix_post, w_out.astype(BF16), layer=0)
    x = _ffn_call(x, mod, g_ffn_pre, g_ffn_post, w1, w3, w2, layer=0)

    conv_w = jnp.concatenate([sc_conv[0], jnp.zeros((SUBLANES_F32 - CONV_WIDTH, D), F32)], axis=0)
    x = _conv_mixer_call(x, mod, g_mix_pre, g_mix_post, sc_w_in[0].astype(BF16), conv_w,
                         sc_w_out[0].astype(BF16), layer=1)
    x = _ffn_call(x, mod, g_ffn_pre, g_ffn_post, w1, w3, w2, layer=1)
    return x
```

```python
import functools

import jax
import jax.numpy as jnp
from jax import lax
from jax.experimental import pallas as pl
from jax.experimental.pallas import tpu as pltpu

D_MODEL = 1024
DEPTH = 2
GRID_W = 64
HEAD_DIM = 64
Q_HEADS = 8
KV_HEADS = 2
GQA = Q_HEADS // KV_HEADS
WINDOW = 128
BLOCK = 128
ATTN_DIM = Q_HEADS * HEAD_DIM
KV_DIM = KV_HEADS * HEAD_DIM
GM_GROUPS = 8
GM_DIM = GM_GROUPS * HEAD_DIM
CHUNK = 128
IN_DIM_EVEN = ATTN_DIM + 2 * KV_DIM + 2 * GM_DIM
CONV_WIDTH = 3
D_FF = 2816
ROPE_THETA = 10000.0
RMS_EPS = 1e-6
LN_EPS = 1e-5
NEG_INF = -1e30
N_MOD = 6

LANES = 128
SUBLANES_F32 = 8
MXU_WIDTH = 256
VMEM_LIMIT_BYTES = 56 * 1024 * 1024

TOKENS_PER_STEP = 512
FF_CHUNK = MXU_WIDTH
FFN_SUB_TILES = 1
MOD_ROWS = 8

F32 = jnp.float32
BF16 = jnp.bfloat16


def _dot(a, b):
    return jnp.dot(a, b, preferred_element_type=F32)


def _dot_nt(a, b):
    return lax.dot_general(a, b, (((1,), (1,)), ((), ())), preferred_element_type=F32)


def _rms_norm(x, g):
    ms = jnp.mean(x * x, axis=-1, keepdims=True)
    return x * lax.rsqrt(ms + RMS_EPS) * g


def _modulate(x, g, shift, scale):
    return _rms_norm(x, g) * (1.0 + scale) + shift


def _silu(x):
    return x * jax.nn.sigmoid(x)


def _split_bf16(x):
    hi = x.astype(BF16)
    lo = (x - hi.astype(F32)).astype(BF16)
    return hi, lo


def _resident(shape):
    nd = len(shape)
    return pl.BlockSpec(shape, lambda *_: (0,) * nd, pipeline_mode=pl.Buffered(1))


def _mod_spec(layer, which):
    return pl.BlockSpec((None, None, MOD_ROWS, D_MODEL), lambda *_: (layer, which, 0, 0))


def _vec_spec(layer):
    return pl.BlockSpec((None, 1, D_MODEL), lambda *_: (layer, 0, 0))


def _params():
    return pltpu.CompilerParams(
        dimension_semantics=("arbitrary", "arbitrary"), vmem_limit_bytes=VMEM_LIMIT_BYTES)


def _mod_kernel(c_ref, w_ref, b_ref, o_ref):
    a_hi, a_lo = _split_bf16(_silu(c_ref[...]))
    w_hi, w_lo = _split_bf16(w_ref[...])
    o_ref[...] = _dot(a_hi, w_hi) + _dot(a_hi, w_lo) + _dot(a_lo, w_hi) + b_ref[...]


def _mod_call(cond, w_mod, b_mod):
    return pl.pallas_call(
        _mod_kernel,
        grid=(DEPTH, N_MOD),
        in_specs=[
            pl.BlockSpec((MOD_ROWS, D_MODEL), lambda l, j: (0, 0)),
            pl.BlockSpec((None, D_MODEL, D_MODEL), lambda l, j: (l, 0, j)),
            pl.BlockSpec((None, None, 1, D_MODEL), lambda l, j: (l, j, 0, 0)),
        ],
        out_specs=pl.BlockSpec((None, None, MOD_ROWS, D_MODEL), lambda l, j: (l, j, 0, 0)),
        out_shape=jax.ShapeDtypeStruct((DEPTH, N_MOD, MOD_ROWS, D_MODEL), F32),
        compiler_params=_params(),
        name="mod_vectors",
    )(cond, w_mod, b_mod.reshape(DEPTH, N_MOD, 1, D_MODEL))


def _rope(slab, cos, sin, first_of_pair):
    up = pltpu.roll(slab, LANES - HEAD_DIM // 4, 1)
    dn = pltpu.roll(slab, HEAD_DIM // 4, 1)
    return slab * cos + jnp.where(first_of_pair, up, dn) * sin


def _token_table(row_ref, col_ref):
    col = col_ref[...]
    return jnp.concatenate(
        [row_ref[r:r + 1, :] + col for r in range(row_ref.shape[0])], axis=0)


def _inproj_even_kernel(x_ref, sh_ref, sc_ref, g_ref, w_ref, rcos_ref, rsin_ref, ccos_ref, csin_ref,
                        p_ref, vnorm_ref, ws_ref, bs_ref, q_ref, k_ref, v_ref, ogm_ref):
    b = pl.program_id(0)
    shift = sh_ref[pl.ds(b, 1), :]
    scale = sc_ref[pl.ds(b, 1), :]
    h = _modulate(x_ref[...], g_ref[...], shift, scale).astype(BF16)
    tm = h.shape[0]

    kv_at, u_at, gv_at = ATTN_DIM, ATTN_DIM + 2 * KV_DIM, IN_DIM_EVEN - GM_DIM
    cos = _token_table(rcos_ref, ccos_ref)
    sin = _token_table(rsin_ref, csin_ref)
    lane = lax.broadcasted_iota(jnp.int32, (tm, LANES), 1)
    first_of_pair = (lane & (HEAD_DIM // 4)) == 0
    q = _dot(h, w_ref[:, :kv_at])
    for j in range(ATTN_DIM // LANES):
        cols = slice(j * LANES, (j + 1) * LANES)
        q_ref[:, cols] = _rope(q[:, cols], cos, sin, first_of_pair).astype(BF16)
    kv = _dot(h, w_ref[:, kv_at:u_at])
    k_ref[...] = _rope(kv[:, :KV_DIM], cos, sin, first_of_pair).astype(BF16)
    v_ref[...] = kv[:, KV_DIM:].astype(BF16)
    u = _dot(h, w_ref[:, u_at:gv_at])

    d = _dot(h, w_ref[:, gv_at:])
    var = _dot((d * d).astype(BF16), p_ref[...])
    vn = (d * lax.rsqrt(var + LN_EPS) * vnorm_ref[...]).astype(BF16)

    low_half = lax.broadcasted_iota(jnp.int32, (CHUNK, LANES), 1) < HEAD_DIM
    zero = jnp.zeros((CHUNK, LANES), BF16)
    for c in range(tm // CHUNK):
        rows = slice(c * CHUNK, (c + 1) * CHUNK)
        for gp in range(GM_GROUPS // 2):
            cols = slice(gp * LANES, (gp + 1) * LANES)
            blk = vn[rows, cols]
            rhs = jnp.concatenate(
                [jnp.where(low_half, blk, zero), jnp.where(low_half, zero, blk)], axis=0)
            s = _dot(ws_ref[gp], rhs) + bs_ref[gp]
            ogm_ref[rows, cols] = (u[rows, cols] * s).astype(BF16)


def _inproj_even_call(x, mod, g_pre, w_in, rope, p_mat, v_norm, ws2, bs2, layer):
    B, S, _ = x.shape
    tm = TOKENS_PER_STEP
    assert tm // GRID_W == SUBLANES_F32
    row_cos, row_sin, col_cos, col_sin = rope
    tok = lambda width: pl.BlockSpec((None, tm, width), lambda b, i: (b, i, 0))
    row_tab = pl.BlockSpec((tm // GRID_W, LANES), lambda b, i: (i, 0))
    return pl.pallas_call(
        _inproj_even_kernel,
        grid=(B, S // tm),
        in_specs=[
            tok(D_MODEL),
            _mod_spec(layer, 0), _mod_spec(layer, 1), _vec_spec(layer),
            _resident((D_MODEL, IN_DIM_EVEN)),
            row_tab, row_tab, _resident((GRID_W, LANES)), _resident((GRID_W, LANES)),
            _resident((GM_DIM, GM_DIM)),
            _resident((1, GM_DIM)),
            _resident((GM_GROUPS // 2, CHUNK, 2 * CHUNK)),
            _resident((GM_GROUPS // 2, CHUNK, LANES)),
        ],
        out_specs=[tok(ATTN_DIM), tok(KV_DIM), tok(KV_DIM), tok(GM_DIM)],
        out_shape=[
            jax.ShapeDtypeStruct((B, S, ATTN_DIM), BF16),
            jax.ShapeDtypeStruct((B, S, KV_DIM), BF16),
            jax.ShapeDtypeStruct((B, S, KV_DIM), BF16),
            jax.ShapeDtypeStruct((B, S, GM_DIM), BF16),
        ],
        compiler_params=_params(),
        name="inproj_even",
    )(x, mod, mod, g_pre, w_in, row_cos, row_sin, col_cos, col_sin, p_mat, v_norm, ws2, bs2)


def _ctx_kv_kernel(ctx_ref, sh_ref, sc_ref, g_ref, w_ref, kc_ref, vc_ref, *, ctx_row):
    shift = sh_ref[ctx_row:ctx_row + 1, :]
    scale = sc_ref[ctx_row:ctx_row + 1, :]
    hc = _modulate(ctx_ref[...], g_ref[...], shift, scale).astype(BF16)
    kv = _dot(hc, w_ref[...])
    kc_ref[...] = kv[:, :KV_DIM].astype(BF16)
    vc_ref[...] = kv[:, KV_DIM:].astype(BF16)


def _ctx_kv_call(ctx, mod, g_pre, w_kv, layer, ctx_row):
    B, L, _ = ctx.shape
    out = pl.BlockSpec((None, L, KV_DIM), lambda b, i: (b, 0, 0))
    return pl.pallas_call(
        functools.partial(_ctx_kv_kernel, ctx_row=ctx_row),
        grid=(B, 1),
        in_specs=[
            pl.BlockSpec((None, L, D_MODEL), lambda b, i: (b, 0, 0)),
            _mod_spec(layer, 0), _mod_spec(layer, 1), _vec_spec(layer),
            _resident((D_MODEL, 2 * KV_DIM)),
        ],
        out_specs=[out, out],
        out_shape=[jax.ShapeDtypeStruct((B, L, KV_DIM), BF16)] * 2,
        compiler_params=_params(),
        name="ctx_kv",
    )(ctx, mod, mod, g_pre, w_kv)


def _attn_out_kernel(sink_ref, q_ref, kp_ref, km_ref, kn_ref, vp_ref, vm_ref, vn_ref, kc_ref, vc_ref,
                     ogm_ref, x_ref, gt_ref, gpost_ref, wo_ref, o_ref, oattn_ref):
    b = pl.program_id(0)
    i = pl.program_id(1)
    n_steps = pl.num_programs(1)
    tq = q_ref.shape[0]
    n_blocks = tq // BLOCK
    win = 3 * BLOCK

    kwin = jnp.concatenate([kp_ref[...], km_ref[...], kn_ref[...]], axis=0)
    vwin = jnp.concatenate([vp_ref[...], vm_ref[...], vn_ref[...]], axis=0)
    kc = kc_ref[...]
    vc = vc_ref[...]
    n_keys = win + kc.shape[0]

    qi = lax.broadcasted_iota(jnp.int32, (BLOCK, win), 0)
    kj = lax.broadcasted_iota(jnp.int32, (BLOCK, win), 1)
    band = jnp.where(jnp.abs(kj - BLOCK - qi) <= WINDOW, 0.0, NEG_INF).astype(F32)
    before_start = jnp.where(kj < BLOCK, jnp.where(i == 0, NEG_INF, 0.0), 0.0).astype(F32)
    after_end = jnp.where(kj >= 2 * BLOCK, jnp.where(i == n_steps - 1, NEG_INF, 0.0), 0.0).astype(F32)

    q_half = [(lax.broadcasted_iota(jnp.int32, (BLOCK, LANES), 1) // HEAD_DIM) == j
              for j in range(KV_HEADS)]
    kv_half = [(lax.broadcasted_iota(jnp.int32, (n_keys, LANES), 1) // HEAD_DIM) == j
               for j in range(KV_HEADS)]

    zero = jnp.zeros((), BF16)
    ones_half = [jnp.where(kv_half[j], 1.0, 0.0).astype(BF16) for j in range(KV_HEADS)]

    def scores(qb):
        q_blk = q_ref[qb * BLOCK:(qb + 1) * BLOCK, :]
        qs = jnp.concatenate([q_blk[:, g * LANES:(g + 1) * LANES] for g in range(GQA)], axis=0)
        k_all = jnp.concatenate([kwin[qb * BLOCK:qb * BLOCK + win, :], kc], axis=0)
        k_heads = jnp.concatenate([jnp.where(kv_half[j], k_all, zero) for j in range(KV_HEADS)], axis=0)
        return _dot_nt(qs, k_heads)

    s_next = scores(0)
    for qb in range(n_blocks):
        rows = slice(qb * BLOCK, (qb + 1) * BLOCK)
        bias = band
        if qb == 0:
            bias = bias + before_start
        if qb == n_blocks - 1:
            bias = bias + after_end
        s = s_next
        if qb + 1 < n_blocks:
            s_next = scores(qb + 1)

        probs, sink_terms = [], []
        for g in range(GQA):
            p_g, t_g = [], []
            for j in range(KV_HEADS):
                sink = sink_ref[j * GQA + g]
                sg = s[g * BLOCK:(g + 1) * BLOCK, j * n_keys:(j + 1) * n_keys]
                sg = jnp.concatenate([sg[:, :win] + bias, sg[:, win:]], axis=1)
                m = jnp.maximum(jnp.max(sg, axis=-1, keepdims=True), sink)
                p_g.append(jnp.exp(sg - m).astype(BF16))
                t_g.append(jnp.broadcast_to(jnp.exp(sink - m), (BLOCK, LANES)))
            probs.append(jnp.concatenate(p_g, axis=1))
            sink_terms.append(jnp.where(q_half[0], t_g[0], t_g[1]))

        v_all = jnp.concatenate([vwin[qb * BLOCK:qb * BLOCK + win, :], vc], axis=0)
        v_heads = jnp.concatenate(
            [jnp.concatenate([jnp.where(kv_half[j], v_all, zero), ones_half[j]], axis=1)
             for j in range(KV_HEADS)], axis=0)
        e = _dot(jnp.concatenate(probs, axis=0), v_heads)
        for g in range(GQA):
            eg = e[g * BLOCK:(g + 1) * BLOCK, :]
            slab = eg[:, :LANES] / (eg[:, LANES:] + sink_terms[g])
            oattn_ref[rows, g * LANES:(g + 1) * LANES] = slab.astype(BF16)

    y = _dot(oattn_ref[...], wo_ref[:ATTN_DIM, :]) + _dot(ogm_ref[...], wo_ref[ATTN_DIM:, :])
    gate = gt_ref[pl.ds(b, 1), :]
    o_ref[...] = x_ref[...] + gate * _rms_norm(y, gpost_ref[...])


def _attn_out_call(sink, q, k, v, kc, vc, ogm, x, mod, g_post, w_out, layer):
    B, S, _ = x.shape
    tq = TOKENS_PER_STEP
    bpt = tq // BLOCK
    n_blk = S // BLOCK
    L = kc.shape[1]
    tok = lambda width: pl.BlockSpec((None, tq, width), lambda b, i: (b, i, 0))
    prev = pl.BlockSpec((None, BLOCK, KV_DIM), lambda b, i: (b, jnp.maximum(i * bpt - 1, 0), 0))
    nxt = pl.BlockSpec((None, BLOCK, KV_DIM),
                       lambda b, i: (b, jnp.minimum((i + 1) * bpt, n_blk - 1), 0))
    ctx = pl.BlockSpec((None, L, KV_DIM), lambda b, i: (b, 0, 0))
    return pl.pallas_call(
        _attn_out_kernel,
        grid=(B, S // tq),
        in_specs=[
            pl.BlockSpec(memory_space=pltpu.SMEM),
            tok(ATTN_DIM),
            prev, tok(KV_DIM), nxt,
            prev, tok(KV_DIM), nxt,
            ctx, ctx,
            tok(GM_DIM), tok(D_MODEL),
            _mod_spec(layer, 2), _vec_spec(layer),
            _resident((ATTN_DIM + GM_DIM, D_MODEL)),
        ],
        out_specs=tok(D_MODEL),
        out_shape=jax.ShapeDtypeStruct((B, S, D_MODEL), F32),
        scratch_shapes=[pltpu.VMEM((tq, ATTN_DIM), BF16)],
        compiler_params=_params(),
        name="attn_out",
    )(sink, q, k, k, k, v, v, v, kc, vc, ogm, x, mod, g_post, w_out)


def _ffn_kernel(x_ref, sh_ref, sc_ref, gt_ref, gpre_ref, gpost_ref, w1_ref, w3_ref, w2_ref, o_ref,
                acc_ref):
    b = pl.program_id(0)
    shift, scale, gate = sh_ref[pl.ds(b, 1), :], sc_ref[pl.ds(b, 1), :], gt_ref[pl.ds(b, 1), :]
    sub = x_ref.shape[0] // FFN_SUB_TILES
    for t in range(FFN_SUB_TILES):
        rows = slice(t * sub, (t + 1) * sub)
        x = x_ref[rows, :]
        hf = _modulate(x, gpre_ref[...], shift, scale).astype(BF16)
        for c in range(D_FF // FF_CHUNK):
            cols = slice(c * FF_CHUNK, (c + 1) * FF_CHUNK)
            gated = (_silu(_dot(hf, w1_ref[:, cols])) * _dot(hf, w3_ref[:, cols])).astype(BF16)
            part = _dot(gated, w2_ref[cols, :])
            if c == 0:
                acc_ref[rows, :] = part
            else:
                acc_ref[rows, :] += part
        o_ref[rows, :] = x + gate * _rms_norm(acc_ref[rows, :], gpost_ref[...])


def _ffn_call(x, mod, g_pre, g_post, w1, w3, w2, layer):
    B, S, _ = x.shape
    tm = TOKENS_PER_STEP
    tok = pl.BlockSpec((None, tm, D_MODEL), lambda b, i: (b, i, 0))
    return pl.pallas_call(
        _ffn_kernel,
        grid=(B, S // tm),
        in_specs=[
            tok,
            _mod_spec(layer, 3), _mod_spec(layer, 4), _mod_spec(layer, 5),
            _vec_spec(layer), _vec_spec(layer),
            pl.BlockSpec((None, D_MODEL, D_FF), lambda b, i: (layer, 0, 0), pipeline_mode=pl.Buffered(1)),
            pl.BlockSpec((None, D_MODEL, D_FF), lambda b, i: (layer, 0, 0), pipeline_mode=pl.Buffered(1)),
            pl.BlockSpec((None, D_FF, D_MODEL), lambda b, i: (layer, 0, 0), pipeline_mode=pl.Buffered(1)),
        ],
        out_specs=tok,
        out_shape=jax.ShapeDtypeStruct((B, S, D_MODEL), F32),
        scratch_shapes=[pltpu.VMEM((tm, D_MODEL), F32)],
        compiler_params=_params(),
        name="swiglu_ffn",
    )(x, mod, mod, mod, g_pre, g_post, w1, w3, w2)


HALO = SUBLANES_F32


def _conv_mixer_kernel(xp_ref, x_ref, xn_ref, sh_ref, sc_ref, gt_ref, gpre_ref, gpost_ref, wi_ref,
                       cw_ref, wo_ref, o_ref):
    b = pl.program_id(0)
    i = pl.program_id(1)
    n_steps = pl.num_programs(1)
    x = x_ref[...]
    tm = x.shape[0]
    ext = jnp.concatenate([xp_ref[...], x, xn_ref[...]], axis=0)
    h = _modulate(ext, gpre_ref[...], sh_ref[pl.ds(b, 1), :], sc_ref[pl.ds(b, 1), :]).astype(BF16)
    y = _dot(h, wi_ref[:, D_MODEL:2 * D_MODEL]) * _dot(h, wi_ref[:, 2 * D_MODEL:])
    r = lax.broadcasted_iota(jnp.int32, (tm + 2 * HALO, 1), 0)
    first_row = jnp.where(i > 0, 0, HALO)
    end_row = jnp.where(i < n_steps - 1, tm + 2 * HALO, tm + HALO)
    inside = (r >= first_row) & (r < end_row)
    y = jnp.where(inside, y, 0.0)
    cw = cw_ref[...]
    yconv = (cw[0:1, :] * y[HALO - 1:HALO - 1 + tm, :] + cw[1:2, :] * y[HALO:HALO + tm, :]
             + cw[2:3, :] * y[HALO + 1:HALO + 1 + tm, :])
    bg = _dot(h[HALO:HALO + tm, :], wi_ref[:, :D_MODEL])
    out = _dot((bg * yconv).astype(BF16), wo_ref[...])
    o_ref[...] = x + gt_ref[pl.ds(b, 1), :] * _rms_norm(out, gpost_ref[...])


def _conv_mixer_call(x, mod, g_pre, g_post, w_in, conv_w, w_out, layer):
    B, S, _ = x.shape
    tm = TOKENS_PER_STEP
    hpt = tm // HALO
    n_halo = S // HALO
    tok = pl.BlockSpec((None, tm, D_MODEL), lambda b, i: (b, i, 0))
    prev = pl.BlockSpec((None, HALO, D_MODEL), lambda b, i: (b, jnp.maximum(i * hpt - 1, 0), 0))
    nxt = pl.BlockSpec((None, HALO, D_MODEL),
                       lambda b, i: (b, jnp.minimum((i + 1) * hpt, n_halo - 1), 0))
    return pl.pallas_call(
        _conv_mixer_kernel,
        grid=(B, S // tm),
        in_specs=[
            prev, tok, nxt,
            _mod_spec(layer, 0), _mod_spec(layer, 1), _mod_spec(layer, 2),
            _vec_spec(layer), _vec_spec(layer),
            _resident((D_MODEL, 3 * D_MODEL)),
            _resident((SUBLANES_F32, D_MODEL)),
            _resident((D_MODEL, D_MODEL)),
        ],
        out_specs=tok,
        out_shape=jax.ShapeDtypeStruct((B, S, D_MODEL), F32),
        compiler_params=_params(),
        name="conv_mixer",
    )(x, x, x, mod, mod, mod, g_pre, g_post, w_in, conv_w, w_out)


def _rope_tables(seq):
    quarter = HEAD_DIM // 4
    rows = seq // GRID_W
    inv = ROPE_THETA ** (-jnp.arange(quarter, dtype=F32) / quarter)
    ra = jnp.arange(rows, dtype=F32)[:, None] * inv
    ca = jnp.arange(GRID_W, dtype=F32)[:, None] * inv
    reps = LANES // HEAD_DIM
    zr, zc = jnp.zeros_like(ra), jnp.zeros_like(ca)
    lanes = lambda parts: jnp.tile(jnp.concatenate(parts, axis=-1), (1, reps))
    row_cos = lanes([jnp.cos(ra), jnp.cos(ra), zr, zr])
    row_sin = lanes([-jnp.sin(ra), jnp.sin(ra), zr, zr])
    col_cos = lanes([zc, zc, jnp.cos(ca), jnp.cos(ca)])
    col_sin = lanes([zc, zc, -jnp.sin(ca), jnp.sin(ca)])
    return row_cos, row_sin, col_cos, col_sin


def _to_head_slabs(w, axis):
    shape = w.shape
    split = shape[:axis] + (KV_HEADS, GQA, HEAD_DIM) + shape[axis + 1:]
    return jnp.swapaxes(w.reshape(split), axis, axis + 1).reshape(shape)


def _prep_even_weights(w_in, ws, bs):
    wq = _to_head_slabs(w_in[:, :ATTN_DIM], 1) * (HEAD_DIM ** -0.5)
    rest = w_in[:, ATTN_DIM:ATTN_DIM + 2 * KV_DIM + GM_DIM]
    wg = w_in[:, IN_DIM_EVEN - GM_DIM:].reshape(D_MODEL, GM_GROUPS, HEAD_DIM)
    wg = (wg - jnp.mean(wg, axis=-1, keepdims=True)).reshape(D_MODEL, GM_DIM)
    w_all = jnp.concatenate([wq, rest, wg], axis=1).astype(BF16)
    w_kv = w_in[:, ATTN_DIM:ATTN_DIM + 2 * KV_DIM].astype(BF16)
    half = GM_GROUPS // 2
    ws2 = ws.reshape(half, 2, CHUNK, CHUNK).transpose(0, 2, 1, 3).reshape(half, CHUNK, 2 * CHUNK)
    bs2 = jnp.repeat(bs.reshape(half, 2, CHUNK).transpose(0, 2, 1), HEAD_DIM, axis=-1)
    return w_all, w_kv, ws2.astype(BF16), bs2.astype(F32)


def kernel(x, c, ctx, c_ctx, w_mod, b_mod, g_mix_pre, g_mix_post, g_ffn_pre, g_ffn_post, ffn_w1, ffn_w3,
           ffn_w2, a_w_in, a_sink, gm_v_norm, gm_ws, gm_bs, a_w_out, sc_w_in, sc_conv, sc_w_out):
    B, S, D = x.shape
    assert D == D_MODEL and S % TOKENS_PER_STEP == 0 and B + 1 <= MOD_ROWS
    assert a_w_in.shape[0] == 1 and sc_w_in.shape[0] == 1 and w_mod.shape[0] == DEPTH

    cond = jnp.concatenate([c, c_ctx[None, :], jnp.zeros((MOD_ROWS - B - 1, D), F32)], axis=0)
    mod = _mod_call(cond, w_mod, b_mod)

    vec = lambda g: g.reshape(DEPTH, 1, D)
    g_mix_pre, g_mix_post, g_ffn_pre, g_ffn_post = map(vec, (g_mix_pre, g_mix_post, g_ffn_pre, g_ffn_post))
    w1, w3, w2 = ffn_w1.astype(BF16), ffn_w3.astype(BF16), ffn_w2.astype(BF16)

    w_all, w_kv, ws2, bs2 = _prep_even_weights(a_w_in[0], gm_ws[0], gm_bs[0])
    group = jnp.arange(GM_DIM) // HEAD_DIM
    p_mat = jnp.where(group[:, None] == group[None, :], 1.0 / HEAD_DIM, 0.0).astype(BF16)
    q, k, v, ogm = _inproj_even_call(x, mod, g_mix_pre, w_all, _rope_tables(S), p_mat,
                                     gm_v_norm[0].reshape(1, GM_DIM), ws2, bs2, layer=0)
    kc, vc = _ctx_kv_call(ctx, mod, g_mix_pre, w_kv, layer=0, ctx_row=B)
    w_out = jnp.concatenate([_to_head_slabs(a_w_out[0][:ATTN_DIM], 0), a_w_out[0][ATTN_DIM:]], axis=0)
    x = _attn_out_call(a_sink[0], q, k, v, kc, vc, ogm, x, mod, g_mix_post, w_out.astype(BF16), layer=0)
    x = _ffn_call(x, mod, g_ffn_pre, g_ffn_post, w1, w3, w2, layer=0)

    conv_w = jnp.concatenate([sc_conv[0], jnp.zeros((SUBLANES_F32 - CONV_WIDTH, D), F32)], axis=0)
    x = _conv_mixer_call(x, mod, g_mix_pre, g_mix_post, sc_w_in[0].astype(BF16), conv_w,
                         sc_w_out[0].astype(BF16), layer=1)
    x = _ffn_call(x, mod, g_ffn_pre, g_ffn_post, w1, w3, w2, layer=1)
    return x
```

```python
import functools

import jax
import jax.numpy as jnp
from jax import lax
from jax.experimental import pallas as pl
from jax.experimental.pallas import tpu as pltpu

D_MODEL = 1024
DEPTH = 2
GRID_W = 64
HEAD_DIM = 64
Q_HEADS = 8
KV_HEADS = 2
GQA = Q_HEADS // KV_HEADS
WINDOW = 128
BLOCK = 128
ATTN_DIM = Q_HEADS * HEAD_DIM
KV_DIM = KV_HEADS * HEAD_DIM
GM_GROUPS = 8
GM_DIM = GM_GROUPS * HEAD_DIM
CHUNK = 128
IN_DIM_EVEN = ATTN_DIM + 2 * KV_DIM + 2 * GM_DIM
CONV_WIDTH = 3
D_FF = 2816
ROPE_THETA = 10000.0
RMS_EPS = 1e-6
LN_EPS = 1e-5
NEG_INF = -1e30
LOG2_E = 1.4426950408889634
N_MOD = 6

LANES = 128
SUBLANES_F32 = 8
MXU_WIDTH = 256
VMEM_LIMIT_BYTES = 56 * 1024 * 1024

TOKENS_PER_STEP = 512
FF_CHUNK = MXU_WIDTH
FFN_SUB_ROWS = (256, 512, 256)
MOD_ROWS = 8

F32 = jnp.float32
BF16 = jnp.bfloat16


def _dot(a, b):
    return jnp.dot(a, b, preferred_element_type=F32)


def _dot_nt(a, b):
    return lax.dot_general(a, b, (((1,), (1,)), ((), ())), preferred_element_type=F32)


def _rms_norm(x, g):
    ms = jnp.mean(x * x, axis=-1, keepdims=True)
    return x * lax.rsqrt(ms + RMS_EPS) * g


def _modulate(x, g, shift, scale):
    return _rms_norm(x, g) * (1.0 + scale) + shift


def _silu(x):
    return x * jax.nn.sigmoid(x)


def _split_bf16(x):
    hi = x.astype(BF16)
    lo = (x - hi.astype(F32)).astype(BF16)
    return hi, lo


def _ordered_after(value, anchor):
    bits = anchor[:SUBLANES_F32, :LANES].astype(jnp.int32)
    zero = lax.shift_right_logical(lax.shift_right_logical(bits, 16), 16).astype(F32)
    head = jnp.concatenate([value[:SUBLANES_F32, :LANES] + zero, value[:SUBLANES_F32, LANES:]], axis=1)
    return jnp.concatenate([head, value[SUBLANES_F32:, :]], axis=0)


def _resident(shape):
    nd = len(shape)
    return pl.BlockSpec(shape, lambda *_: (0,) * nd, pipeline_mode=pl.Buffered(1))


def _mod_spec(layer, which):
    return pl.BlockSpec((None, None, MOD_ROWS, D_MODEL), lambda *_: (layer, which, 0, 0))


MIX_PRE, MIX_POST, FFN_PRE, FFN_POST = range(4)


def _vec_spec(which, layer):
    return pl.BlockSpec((None, 1, D_MODEL), lambda *_: (which * DEPTH + layer, 0, 0))


def _params():
    return pltpu.CompilerParams(
        dimension_semantics=("arbitrary", "arbitrary"), vmem_limit_bytes=VMEM_LIMIT_BYTES)


def _mod_kernel(c_ref, w_ref, b_ref, o_ref):
    a_hi, a_lo = _split_bf16(_silu(c_ref[...]))
    w_hi, w_lo = _split_bf16(w_ref[...])
    o_ref[...] = _dot(a_hi, w_hi) + _dot(a_hi, w_lo) + _dot(a_lo, w_hi) + b_ref[...]


def _mod_call(cond, w_mod, b_mod):
    return pl.pallas_call(
        _mod_kernel,
        grid=(DEPTH, N_MOD),
        in_specs=[
            pl.BlockSpec((MOD_ROWS, D_MODEL), lambda l, j: (0, 0)),
            pl.BlockSpec((None, D_MODEL, D_MODEL), lambda l, j: (l, 0, j)),
            pl.BlockSpec((None, None, 1, D_MODEL), lambda l, j: (l, j, 0, 0)),
        ],
        out_specs=pl.BlockSpec((None, None, MOD_ROWS, D_MODEL), lambda l, j: (l, j, 0, 0)),
        out_shape=jax.ShapeDtypeStruct((DEPTH, N_MOD, MOD_ROWS, D_MODEL), F32),
        compiler_params=_params(),
        name="mod_vectors",
    )(cond, w_mod, b_mod.reshape(DEPTH, N_MOD, 1, D_MODEL))


def _rope(slab, cos, sin, first_of_pair):
    up = pltpu.roll(slab, LANES - HEAD_DIM // 4, 1)
    dn = pltpu.roll(slab, HEAD_DIM // 4, 1)
    return slab * cos + jnp.where(first_of_pair, up, dn) * sin


def _token_table(rows, col):
    return jnp.concatenate([rows[r:r + 1, :] + col for r in range(rows.shape[0])], axis=0)


INPROJ_SUB_TILES = 4
N_LATER_WEIGHTS = 5


def _inproj_even_kernel(x_ref, sh_ref, sc_ref, g_ref, w_ref, rcos_ref, rsin_ref, ccos_ref, csin_ref,
                        p_ref, vnorm_ref, ws_ref, bs_ref, *rest):
    later_f32 = rest[:N_LATER_WEIGHTS]
    q_ref, k_ref, v_ref, ogm_ref = rest[N_LATER_WEIGHTS:N_LATER_WEIGHTS + 4]
    later_bf16 = rest[N_LATER_WEIGHTS + 4:2 * N_LATER_WEIGHTS + 4]
    h_ref = rest[-1]
    for src, dst in zip(later_f32, later_bf16):
        dst[...] = src[...].astype(BF16)

    b = pl.program_id(0)
    shift = sh_ref[pl.ds(b, 1), :]
    scale = sc_ref[pl.ds(b, 1), :]
    n_sub, sub, _ = h_ref.shape
    n_stages = 4
    pieces = _row_pieces(sub, n_stages, 2 * SUBLANES_F32)
    grid_rows = sub // GRID_W

    kv_at, u_at, gv_at = ATTN_DIM, ATTN_DIM + 2 * KV_DIM, IN_DIM_EVEN - GM_DIM
    lane = lax.broadcasted_iota(jnp.int32, (sub, LANES), 1)
    first_of_pair = (lane & (HEAD_DIM // 4)) == 0
    low_half = lax.broadcasted_iota(jnp.int32, (CHUNK, LANES), 1) < HEAD_DIM
    zero = jnp.zeros((CHUNK, LANES), BF16)
    n_chunks = sub // CHUNK

    def pre_norm(t, r0, r1, after=None):
        x = x_ref[t * sub + r0:t * sub + r1, :]
        if after is not None:
            x = _ordered_after(x, after)
        h_ref[t, r0:r1, :] = _modulate(x, g_ref[...], shift, scale).astype(BF16)

    pre_norm(0, 0, sub)
    for t in range(n_sub):
        def next_piece(stage, anchor):
            if t + 1 < n_sub:
                pre_norm(t + 1, *pieces[stage], after=anchor)

        rows_t = slice(t * sub, (t + 1) * sub)
        h = h_ref[t]
        cos = _token_table(rcos_ref[t * grid_rows:(t + 1) * grid_rows, :], ccos_ref[...])
        sin = _token_table(rsin_ref[t * grid_rows:(t + 1) * grid_rows, :], csin_ref[...])

        d = _dot(h, w_ref[:, gv_at:])
        next_piece(0, d)
        q = _dot(h, w_ref[:, :kv_at])
        var = _dot((d * d).astype(BF16), p_ref[...])
        for j in range(ATTN_DIM // LANES):
            cols = slice(j * LANES, (j + 1) * LANES)
            q_ref[rows_t, cols] = _rope(q[:, cols], cos, sin, first_of_pair).astype(BF16)
        next_piece(1, q)
        kv = _dot(h, w_ref[:, kv_at:u_at])
        vn = (d * lax.rsqrt(var + LN_EPS) * vnorm_ref[...]).astype(BF16)
        k_ref[rows_t, :] = _rope(kv[:, :KV_DIM], cos, sin, first_of_pair).astype(BF16)
        v_ref[rows_t, :] = kv[:, KV_DIM:].astype(BF16)
        u = _dot(h, w_ref[:, u_at:gv_at])
        next_piece(2, u)

        for gp in range(GM_GROUPS // 2):
            cols = slice(gp * LANES, (gp + 1) * LANES)
            blocks = []
            for c in range(n_chunks):
                blk = vn[c * CHUNK:(c + 1) * CHUNK, cols]
                blocks.append(jnp.concatenate(
                    [jnp.where(low_half, blk, zero), jnp.where(low_half, zero, blk)], axis=0))
            s = _dot(ws_ref[gp], jnp.concatenate(blocks, axis=1))
            if gp == 0:
                next_piece(3, s)
            for c in range(n_chunks):
                rows = slice(c * CHUNK, (c + 1) * CHUNK)
                gate = s[:, c * LANES:(c + 1) * LANES] + bs_ref[gp]
                ogm_ref[t * sub + c * CHUNK:t * sub + (c + 1) * CHUNK, cols] = (
                    u[rows, cols] * gate).astype(BF16)


def _inproj_even_call(x, mod, gains, w_in, rope, p_mat, v_norm, ws2, bs2, later_weights, layer):
    B, S, _ = x.shape
    tm = INPROJ_SUB_TILES * TOKENS_PER_STEP
    assert S % tm == 0 and TOKENS_PER_STEP % GRID_W == 0 and (tm // GRID_W) % SUBLANES_F32 == 0
    row_cos, row_sin, col_cos, col_sin = rope
    tok = lambda width: pl.BlockSpec((None, tm, width), lambda b, i: (b, i, 0))
    row_tab = pl.BlockSpec((tm // GRID_W, LANES), lambda b, i: (i, 0))
    n_steps = B * (S // tm)
    assert len(later_weights) == N_LATER_WEIGHTS
    slices = []
    for w in later_weights:
        rows = w.shape[0] // n_steps
        assert rows * n_steps == w.shape[0] and rows % (2 * SUBLANES_F32) == 0
        slices.append(pl.BlockSpec((rows, w.shape[1]), lambda b, i: (b * (S // tm) + i, 0)))
    outs = pl.pallas_call(
        _inproj_even_kernel,
        grid=(B, S // tm),
        in_specs=[
            tok(D_MODEL),
            _mod_spec(layer, 0), _mod_spec(layer, 1), _vec_spec(MIX_PRE, layer),
            _resident((D_MODEL, IN_DIM_EVEN)),
            row_tab, row_tab, _resident((GRID_W, LANES)), _resident((GRID_W, LANES)),
            _resident((GM_DIM, GM_DIM)),
            _resident((1, GM_DIM)),
            _resident((GM_GROUPS // 2, CHUNK, 2 * CHUNK)),
            _resident((GM_GROUPS // 2, CHUNK, LANES)),
        ] + slices,
        out_specs=[tok(ATTN_DIM), tok(KV_DIM), tok(KV_DIM), tok(GM_DIM)] + slices,
        out_shape=[
            jax.ShapeDtypeStruct((B, S, ATTN_DIM), BF16),
            jax.ShapeDtypeStruct((B, S, KV_DIM), BF16),
            jax.ShapeDtypeStruct((B, S, KV_DIM), BF16),
            jax.ShapeDtypeStruct((B, S, GM_DIM), BF16),
        ] + [jax.ShapeDtypeStruct(w.shape, BF16) for w in later_weights],
        scratch_shapes=[pltpu.VMEM((INPROJ_SUB_TILES, TOKENS_PER_STEP, D_MODEL), BF16)],
        compiler_params=_params(),
        name="inproj_even",
    )(x, mod, mod, gains, w_in, row_cos, row_sin, col_cos, col_sin, p_mat, v_norm, ws2, bs2,
      *later_weights)
    return outs[:4], outs[4:]


def _ctx_kv_kernel(ctx_ref, sh_ref, sc_ref, g_ref, w_ref, kc_ref, vc_ref, *, ctx_row):
    shift = sh_ref[ctx_row:ctx_row + 1, :]
    scale = sc_ref[ctx_row:ctx_row + 1, :]
    hc = _modulate(ctx_ref[...], g_ref[...], shift, scale).astype(BF16)
    kv = _dot(hc, w_ref[...])
    kc_ref[...] = kv[:, :KV_DIM].astype(BF16)
    vc_ref[...] = kv[:, KV_DIM:].astype(BF16)


def _ctx_kv_call(ctx, mod, gains, w_kv, layer, ctx_row):
    B, L, _ = ctx.shape
    out = pl.BlockSpec((None, L, KV_DIM), lambda b, i: (b, 0, 0))
    return pl.pallas_call(
        functools.partial(_ctx_kv_kernel, ctx_row=ctx_row),
        grid=(B, 1),
        in_specs=[
            pl.BlockSpec((None, L, D_MODEL), lambda b, i: (b, 0, 0)),
            _mod_spec(layer, 0), _mod_spec(layer, 1), _vec_spec(MIX_PRE, layer),
            _resident((D_MODEL, 2 * KV_DIM)),
        ],
        out_specs=[out, out],
        out_shape=[jax.ShapeDtypeStruct((B, L, KV_DIM), BF16)] * 2,
        compiler_params=_params(),
        name="ctx_kv",
    )(ctx, mod, mod, gains, w_kv)


ATTN_SUB_TILES = 2


def _attn_out_kernel(sink_ref, q_ref, kp_ref, km_ref, kn_ref, vp_ref, vm_ref, vn_ref, kc_ref, vc_ref,
                     ogm_ref, x_ref, gt_ref, gpost_ref, wo_ref, o_ref, oattn_ref, y_ref):
    b = pl.program_id(0)
    i = pl.program_id(1)
    n_steps = pl.num_programs(1)
    tq = q_ref.shape[0]
    n_blocks = tq // BLOCK
    win = 3 * BLOCK

    kwin = jnp.concatenate([kp_ref[...], km_ref[...], kn_ref[...]], axis=0)
    vwin = jnp.concatenate([vp_ref[...], vm_ref[...], vn_ref[...]], axis=0)
    kc = kc_ref[...]
    vc = vc_ref[...]
    n_keys = win + kc.shape[0]

    qi = lax.broadcasted_iota(jnp.int32, (BLOCK, win), 0)
    kj = lax.broadcasted_iota(jnp.int32, (BLOCK, win), 1)
    band = jnp.where(jnp.abs(kj - BLOCK - qi) <= WINDOW, 0.0, NEG_INF).astype(F32)
    before_start = jnp.where(kj < BLOCK, jnp.where(i == 0, NEG_INF, 0.0), 0.0).astype(F32)
    after_end = jnp.where(kj >= 2 * BLOCK, jnp.where(i == n_steps - 1, NEG_INF, 0.0), 0.0).astype(F32)

    q_half = [(lax.broadcasted_iota(jnp.int32, (BLOCK, LANES), 1) // HEAD_DIM) == j
              for j in range(KV_HEADS)]
    kv_half = [(lax.broadcasted_iota(jnp.int32, (n_keys, LANES), 1) // HEAD_DIM) == j
               for j in range(KV_HEADS)]

    zero = jnp.zeros((), BF16)
    ones_half = [jnp.where(kv_half[j], 1.0, 0.0).astype(BF16) for j in range(KV_HEADS)]

    def scores(qb):
        q_blk = q_ref[qb * BLOCK:(qb + 1) * BLOCK, :]
        qs = jnp.concatenate([q_blk[:, g * LANES:(g + 1) * LANES] for g in range(GQA)], axis=0)
        k_all = jnp.concatenate([kwin[qb * BLOCK:qb * BLOCK + win, :], kc], axis=0)
        k_heads = jnp.concatenate([jnp.where(kv_half[j], k_all, zero) for j in range(KV_HEADS)], axis=0)
        return _dot_nt(qs, k_heads)

    n_sub, sub, _ = y_ref.shape
    blocks_per_sub = sub // BLOCK
    post_pieces = _row_pieces(sub, blocks_per_sub, SUBLANES_F32)
    gate = gt_ref[pl.ds(b, 1), :]

    def post_norm(t, r0, r1):
        rows = slice(t * sub + r0, t * sub + r1)
        out = x_ref[rows, :] + gate * _rms_norm(y_ref[t, r0:r1, :], gpost_ref[...])
        o_ref[rows, :] = out
        return out

    done = None
    s_next = scores(0)
    for qb in range(n_blocks):
        t, qq = divmod(qb, blocks_per_sub)
        rows = slice(qb * BLOCK, (qb + 1) * BLOCK)
        bias = band
        if qb == 0:
            bias = bias + before_start
        if qb == n_blocks - 1:
            bias = bias + after_end
        s = s_next
        if done is not None:
            s = _ordered_after(s, done)
            done = None
        if qb + 1 < n_blocks:
            s_next = scores(qb + 1)

        probs, sink_terms = [], []
        for g in range(GQA):
            p_g, t_g = [], []
            for j in range(KV_HEADS):
                sink = sink_ref[0, j * GQA + g] * LOG2_E
                sg = s[g * BLOCK:(g + 1) * BLOCK, j * n_keys:(j + 1) * n_keys]
                sg = jnp.concatenate(
                    [sg[:, :BLOCK] + bias[:, :BLOCK], sg[:, BLOCK:2 * BLOCK],
                     sg[:, 2 * BLOCK:win] + bias[:, 2 * BLOCK:], sg[:, win:]], axis=1)
                m = jnp.maximum(jnp.max(sg, axis=-1, keepdims=True), sink)
                p_g.append(jnp.exp2(sg - m).astype(BF16))
                t_g.append(jnp.broadcast_to(jnp.exp2(sink - m), (BLOCK, LANES)))
            probs.append(jnp.concatenate(p_g, axis=1))
            sink_terms.append(jnp.where(q_half[0], t_g[0], t_g[1]))

        v_all = jnp.concatenate([vwin[qb * BLOCK:qb * BLOCK + win, :], vc], axis=0)
        v_heads = jnp.concatenate(
            [jnp.concatenate([jnp.where(kv_half[j], v_all, zero), ones_half[j]], axis=1)
             for j in range(KV_HEADS)], axis=0)
        e = _dot(jnp.concatenate(probs, axis=0), v_heads)
        for g in range(GQA):
            eg = e[g * BLOCK:(g + 1) * BLOCK, :]
            slab = eg[:, :LANES] / (eg[:, LANES:] + sink_terms[g])
            oattn_ref[rows, g * LANES:(g + 1) * LANES] = slab.astype(BF16)

        if t > 0:
            done = post_norm(t - 1, *post_pieces[qq])
        if qq == blocks_per_sub - 1:
            sub_rows = slice(t * sub, (t + 1) * sub)
            y_ref[t] = (_dot(oattn_ref[sub_rows, :], wo_ref[:ATTN_DIM, :])
                        + _dot(ogm_ref[sub_rows, :], wo_ref[ATTN_DIM:, :]))
    post_norm(n_sub - 1, 0, sub)


def _attn_out_call(sink, q, k, v, kc, vc, ogm, x, mod, gains, w_out, layer):
    B, S, _ = x.shape
    tq = ATTN_SUB_TILES * TOKENS_PER_STEP
    assert S % tq == 0
    bpt = tq // BLOCK
    n_blk = S // BLOCK
    L = kc.shape[1]
    tok = lambda width: pl.BlockSpec((None, tq, width), lambda b, i: (b, i, 0))
    prev = pl.BlockSpec((None, BLOCK, KV_DIM), lambda b, i: (b, jnp.maximum(i * bpt - 1, 0), 0))
    nxt = pl.BlockSpec((None, BLOCK, KV_DIM),
                       lambda b, i: (b, jnp.minimum((i + 1) * bpt, n_blk - 1), 0))
    ctx = pl.BlockSpec((None, L, KV_DIM), lambda b, i: (b, 0, 0))
    return pl.pallas_call(
        _attn_out_kernel,
        grid=(B, S // tq),
        in_specs=[
            pl.BlockSpec(memory_space=pltpu.SMEM),
            tok(ATTN_DIM),
            prev, tok(KV_DIM), nxt,
            prev, tok(KV_DIM), nxt,
            ctx, ctx,
            tok(GM_DIM), tok(D_MODEL),
            _mod_spec(layer, 2), _vec_spec(MIX_POST, layer),
            _resident((ATTN_DIM + GM_DIM, D_MODEL)),
        ],
        out_specs=tok(D_MODEL),
        out_shape=jax.ShapeDtypeStruct((B, S, D_MODEL), F32),
        scratch_shapes=[pltpu.VMEM((tq, ATTN_DIM), BF16),
                        pltpu.VMEM((ATTN_SUB_TILES, TOKENS_PER_STEP, D_MODEL), F32)],
        compiler_params=_params(),
        name="attn_out",
    )(sink, q, k, k, k, v, v, v, kc, vc, ogm, x, mod, gains, w_out)


def _row_pieces(n_rows, n_pieces, align):
    units = n_rows // align
    assert units * align == n_rows and units >= n_pieces
    bounds = [align * ((units * p) // n_pieces) for p in range(n_pieces + 1)]
    return list(zip(bounds[:-1], bounds[1:]))


def _ffn_kernel(x_ref, sh_ref, sc_ref, gt_ref, gpre_ref, gpost_ref, w1_ref, w3_ref, w2_ref, o_ref,
                hf_ref, acc_ref, g_ref):
    b = pl.program_id(0)
    shift, scale, gate = sh_ref[pl.ds(b, 1), :], sc_ref[pl.ds(b, 1), :], gt_ref[pl.ds(b, 1), :]
    sizes = FFN_SUB_ROWS
    starts = [sum(sizes[:t]) for t in range(len(sizes))]
    n_sub = len(sizes)
    n_chunks = D_FF // FF_CHUNK
    pieces = [_row_pieces(sz, n_chunks, 2 * SUBLANES_F32) for sz in sizes]

    def pre_norm(t, r0, r1, after=None):
        x = x_ref[starts[t] + r0:starts[t] + r1, :]
        if after is not None:
            x = _ordered_after(x, after)
        hf_ref[t, r0:r1, :] = _modulate(x, gpre_ref[...], shift, scale).astype(BF16)

    def post_norm(t, r0, r1):
        rows = slice(starts[t] + r0, starts[t] + r1)
        out = x_ref[rows, :] + gate * _rms_norm(acc_ref[t, r0:r1, :], gpost_ref[...])
        o_ref[rows, :] = out
        return out

    pre_norm(0, 0, sizes[0])
    for t in range(n_sub):
        sz = sizes[t]
        done = None
        for c in range(n_chunks):
            cols = slice(c * FF_CHUNK, (c + 1) * FF_CHUNK)
            hf = hf_ref[t, :sz, :]
            up = _dot(hf, w1_ref[:, cols])
            if done is not None:
                up = _ordered_after(up, done)
            g_ref[t, :sz, cols] = (_silu(up) * _dot(hf, w3_ref[:, cols])).astype(BF16)
            if t + 1 < n_sub:
                pre_norm(t + 1, *pieces[t + 1][c], after=up)
            if t > 0:
                done = post_norm(t - 1, *pieces[t - 1][c])
        acc_ref[t, :sz, :] = _dot(g_ref[t, :sz, :], w2_ref[...])
    post_norm(n_sub - 1, 0, sizes[-1])


def _ffn_call(x, mod, gains, w1, w3, w2, layer):
    B, S, _ = x.shape
    tm = sum(FFN_SUB_ROWS)
    assert S % tm == 0
    tok = pl.BlockSpec((None, tm, D_MODEL), lambda b, i: (b, i, 0))
    slot = (len(FFN_SUB_ROWS), max(FFN_SUB_ROWS))
    return pl.pallas_call(
        _ffn_kernel,
        grid=(B, S // tm),
        in_specs=[
            tok,
            _mod_spec(layer, 3), _mod_spec(layer, 4), _mod_spec(layer, 5),
            _vec_spec(FFN_PRE, layer), _vec_spec(FFN_POST, layer),
            pl.BlockSpec((None, D_MODEL, D_FF), lambda b, i: (layer, 0, 0), pipeline_mode=pl.Buffered(1)),
            pl.BlockSpec((None, D_MODEL, D_FF), lambda b, i: (layer, 0, 0), pipeline_mode=pl.Buffered(1)),
            pl.BlockSpec((None, D_FF, D_MODEL), lambda b, i: (layer, 0, 0), pipeline_mode=pl.Buffered(1)),
        ],
        out_specs=tok,
        out_shape=jax.ShapeDtypeStruct((B, S, D_MODEL), F32),
        scratch_shapes=[pltpu.VMEM(slot + (D_MODEL,), BF16), pltpu.VMEM(slot + (D_MODEL,), F32),
                        pltpu.VMEM(slot + (D_FF,), BF16)],
        compiler_params=_params(),
        name="swiglu_ffn",
    )(x, mod, mod, mod, gains, gains, w1, w3, w2)


HALO = SUBLANES_F32

CONV_SUB_ROWS = (512, 512)


def _conv_mixer_kernel(xp_ref, x_ref, xn_ref, sh_ref, sc_ref, gt_ref, gpre_ref, gpost_ref, wi_ref,
                       cw_ref, wo_ref, o_ref, h_ref, acc_ref, y_ref):
    b = pl.program_id(0)
    i = pl.program_id(1)
    n_steps = pl.num_programs(1)
    shift, scale, gate = sh_ref[pl.ds(b, 1), :], sc_ref[pl.ds(b, 1), :], gt_ref[pl.ds(b, 1), :]
    tm = x_ref.shape[0]
    sizes = CONV_SUB_ROWS
    starts = [sum(sizes[:t]) for t in range(len(sizes))]
    n_sub = len(sizes)
    exts = [sz + 2 * HALO for sz in sizes]
    pre_pieces = [_row_pieces(e, 3, 2 * SUBLANES_F32) for e in exts]

    def ext_rows(t, r0, r1):
        lo, hi = starts[t] + r0 - HALO, starts[t] + r1 - HALO
        parts = [x_ref[max(lo, 0):min(hi, tm), :]]
        if lo < 0:
            assert lo == -HALO
            parts.insert(0, xp_ref[...])
        if hi > tm:
            assert hi == tm + HALO
            parts.append(xn_ref[...])
        return parts[0] if len(parts) == 1 else jnp.concatenate(parts, axis=0)

    def pre_norm(t, r0, r1, after=None):
        x = ext_rows(t, r0, r1)
        if after is not None:
            x = _ordered_after(x, after)
        h_ref[t, r0:r1, :] = _modulate(x, gpre_ref[...], shift, scale).astype(BF16)

    def post_norm(t):
        rows = slice(starts[t], starts[t] + sizes[t])
        out = x_ref[rows, :] + gate * _rms_norm(acc_ref[t, :sizes[t], :], gpost_ref[...])
        o_ref[rows, :] = out

    first_row = jnp.where(i > 0, 0, HALO)
    end_row = jnp.where(i < n_steps - 1, HALO, 0)

    def project(t):
        outs = []
        for k in range(3):
            if t + 1 < n_sub:
                pre_norm(t + 1, *pre_pieces[t + 1][k], after=outs[-1] if outs else None)
            outs.append(_dot(h_ref[t, :exts[t], :], wi_ref[:, k * D_MODEL:(k + 1) * D_MODEL]))
        return outs

    def mix(t, bg, cg, hx):
        sub, ext = sizes[t], exts[t]
        y_ref[t, :ext, :] = cg * hx
        if t == 0:
            keep = lax.broadcasted_iota(jnp.int32, (HALO, 1), 0) >= first_row
            y_ref[t, :HALO, :] = jnp.where(keep, y_ref[t, :HALO, :], 0.0)
        if t == n_sub - 1:
            keep = lax.broadcasted_iota(jnp.int32, (HALO, 1), 0) < end_row
            y_ref[t, ext - HALO:ext, :] = jnp.where(keep, y_ref[t, ext - HALO:ext, :], 0.0)
        cw = cw_ref[...]
        yconv = (cw[0:1, :] * y_ref[t, HALO - 1:HALO - 1 + sub, :]
                 + cw[1:2, :] * y_ref[t, HALO:HALO + sub, :]
                 + cw[2:3, :] * y_ref[t, HALO + 1:HALO + 1 + sub, :])
        return (bg[HALO:HALO + sub, :] * yconv).astype(BF16)

    pre_norm(0, 0, exts[0])
    projected = project(0)
    for t in range(n_sub):
        current = projected
        if t + 1 < n_sub:
            projected = project(t + 1)
        z = mix(t, *current)
        if t > 0:
            post_norm(t - 1)
        acc_ref[t, :sizes[t], :] = _dot(z, wo_ref[...])
    post_norm(n_sub - 1)


def _conv_mixer_call(x, mod, gains, w_in, conv_w, w_out, layer):
    B, S, _ = x.shape
    tm = sum(CONV_SUB_ROWS)
    assert S % tm == 0
    slot = (len(CONV_SUB_ROWS), max(CONV_SUB_ROWS))
    hpt = tm // HALO
    n_halo = S // HALO
    tok = pl.BlockSpec((None, tm, D_MODEL), lambda b, i: (b, i, 0))
    prev = pl.BlockSpec((None, HALO, D_MODEL), lambda b, i: (b, jnp.maximum(i * hpt - 1, 0), 0))
    nxt = pl.BlockSpec((None, HALO, D_MODEL),
                       lambda b, i: (b, jnp.minimum((i + 1) * hpt, n_halo - 1), 0))
    return pl.pallas_call(
        _conv_mixer_kernel,
        grid=(B, S // tm),
        in_specs=[
            prev, tok, nxt,
            _mod_spec(layer, 0), _mod_spec(layer, 1), _mod_spec(layer, 2),
            _vec_spec(MIX_PRE, layer), _vec_spec(MIX_POST, layer),
            _resident((D_MODEL, 3 * D_MODEL)),
            _resident((CONV_WIDTH, D_MODEL)),
            _resident((D_MODEL, D_MODEL)),
        ],
        out_specs=tok,
        out_shape=jax.ShapeDtypeStruct((B, S, D_MODEL), F32),
        scratch_shapes=[pltpu.VMEM((slot[0], slot[1] + 2 * HALO, D_MODEL), BF16),
                        pltpu.VMEM(slot + (D_MODEL,), F32),
                        pltpu.VMEM((slot[0], slot[1] + 2 * HALO, D_MODEL), F32)],
        compiler_params=_params(),
        name="conv_mixer",
    )(x, x, x, mod, mod, mod, gains, gains, w_in, conv_w, w_out)


def _rope_tables(seq):
    quarter = HEAD_DIM // 4
    rows = seq // GRID_W
    inv = ROPE_THETA ** (-jnp.arange(quarter, dtype=F32) / quarter)
    ra = jnp.arange(rows, dtype=F32)[:, None] * inv
    ca = jnp.arange(GRID_W, dtype=F32)[:, None] * inv
    reps = LANES // HEAD_DIM
    zr, zc = jnp.zeros_like(ra), jnp.zeros_like(ca)
    lanes = lambda parts: jnp.tile(jnp.concatenate(parts, axis=-1), (1, reps))
    row_cos = lanes([jnp.cos(ra), jnp.cos(ra), zr, zr])
    row_sin = lanes([-jnp.sin(ra), jnp.sin(ra), zr, zr])
    col_cos = lanes([zc, zc, jnp.cos(ca), jnp.cos(ca)])
    col_sin = lanes([zc, zc, -jnp.sin(ca), jnp.sin(ca)])
    return row_cos, row_sin, col_cos, col_sin


def _to_head_slabs(w, axis):
    shape = w.shape
    split = shape[:axis] + (KV_HEADS, GQA, HEAD_DIM) + shape[axis + 1:]
    return jnp.swapaxes(w.reshape(split), axis, axis + 1).reshape(shape)


def _prep_even_weights(w_in, ws, bs):
    wq = _to_head_slabs(w_in[:, :ATTN_DIM], 1) * (HEAD_DIM ** -0.5 * LOG2_E)
    rest = w_in[:, ATTN_DIM:ATTN_DIM + 2 * KV_DIM + GM_DIM]
    wg = w_in[:, IN_DIM_EVEN - GM_DIM:].reshape(D_MODEL, GM_GROUPS, HEAD_DIM)
    wg = (wg - jnp.mean(wg, axis=-1, keepdims=True)).reshape(D_MODEL, GM_DIM)
    w_all = jnp.concatenate([wq, rest, wg], axis=1).astype(BF16)
    w_kv = w_in[:, ATTN_DIM:ATTN_DIM + 2 * KV_DIM].astype(BF16)
    half = GM_GROUPS // 2
    ws2 = ws.reshape(half, 2, CHUNK, CHUNK).transpose(0, 2, 1, 3).reshape(half, CHUNK, 2 * CHUNK)
    bs2 = jnp.repeat(bs.reshape(half, 2, CHUNK).transpose(0, 2, 1), HEAD_DIM, axis=-1)
    return w_all, w_kv, ws2.astype(BF16), bs2.astype(F32)


def kernel(x, c, ctx, c_ctx, w_mod, b_mod, g_mix_pre, g_mix_post, g_ffn_pre, g_ffn_post, ffn_w1, ffn_w3,
           ffn_w2, a_w_in, a_sink, gm_v_norm, gm_ws, gm_bs, a_w_out, sc_w_in, sc_conv, sc_w_out):
    B, S, D = x.shape
    assert D == D_MODEL and S % TOKENS_PER_STEP == 0 and B + 1 <= MOD_ROWS
    assert a_w_in.shape[0] == 1 and sc_w_in.shape[0] == 1 and w_mod.shape[0] == DEPTH

    cond = jnp.concatenate([c, c_ctx[None, :], jnp.zeros((MOD_ROWS - B - 1, D), F32)], axis=0)
    mod = _mod_call(cond, w_mod, b_mod)

    gains = jnp.stack([g_mix_pre, g_mix_post, g_ffn_pre, g_ffn_post]).reshape(4 * DEPTH, 1, D)
    later = (ffn_w1.reshape(DEPTH * D, D_FF), ffn_w3.reshape(DEPTH * D, D_FF),
             ffn_w2.reshape(DEPTH * D_FF, D), sc_w_in[0], sc_w_out[0])

    w_all, w_kv, ws2, bs2 = _prep_even_weights(a_w_in[0], gm_ws[0], gm_bs[0])
    group = jnp.arange(GM_DIM) // HEAD_DIM
    p_mat = jnp.where(group[:, None] == group[None, :], 1.0 / HEAD_DIM, 0.0).astype(BF16)
    (q, k, v, ogm), later = _inproj_even_call(
        x, mod, gains, w_all, _rope_tables(S), p_mat, gm_v_norm, ws2, bs2,
        later, layer=0)
    w1, w3 = later[0].reshape(DEPTH, D, D_FF), later[1].reshape(DEPTH, D, D_FF)
    w2, sc_in, sc_out = later[2].reshape(DEPTH, D_FF, D), later[3], later[4]
    kc, vc = _ctx_kv_call(ctx, mod, gains, w_kv, layer=0, ctx_row=B)
    w_out = jnp.concatenate([_to_head_slabs(a_w_out[0][:ATTN_DIM], 0), a_w_out[0][ATTN_DIM:]], axis=0)
    x = _attn_out_call(a_sink, q, k, v, kc, vc, ogm, x, mod, gains, w_out.astype(BF16), layer=0)
    x = _ffn_call(x, mod, gains, w1, w3, w2, layer=0)

    x = _conv_mixer_call(x, mod, gains, sc_in, sc_conv[0], sc_out, layer=1)
    x = _ffn_call(x, mod, gains, w1, w3, w2, layer=1)
    return x
```

```python
import functools

import jax
import jax.numpy as jnp
from jax import lax
from jax.experimental import pallas as pl
from jax.experimental.pallas import tpu as pltpu

D_MODEL = 1024
DEPTH = 2
GRID_W = 64
HEAD_DIM = 64
Q_HEADS = 8
KV_HEADS = 2
GQA = Q_HEADS // KV_HEADS
WINDOW = 128
BLOCK = 128
ATTN_DIM = Q_HEADS * HEAD_DIM
KV_DIM = KV_HEADS * HEAD_DIM
GM_GROUPS = 8
GM_DIM = GM_GROUPS * HEAD_DIM
CHUNK = 128
IN_DIM_EVEN = ATTN_DIM + 2 * KV_DIM + 2 * GM_DIM
CONV_WIDTH = 3
D_FF = 2816
ROPE_THETA = 10000.0
RMS_EPS = 1e-6
LN_EPS = 1e-5
NEG_INF = -1e30
LOG2_E = 1.4426950408889634
N_MOD = 6

LANES = 128
SUBLANES_F32 = 8
MXU_WIDTH = 256
VMEM_LIMIT_BYTES = 56 * 1024 * 1024

TOKENS_PER_STEP = 512
FF_CHUNK = MXU_WIDTH
FFN_SUB_ROWS = (256, 512, 256)
MOD_ROWS = 8

F32 = jnp.float32
BF16 = jnp.bfloat16


def _dot(a, b):
    return jnp.dot(a, b, preferred_element_type=F32)


def _dot_nt(a, b):
    return lax.dot_general(a, b, (((1,), (1,)), ((), ())), preferred_element_type=F32)


def _rms_norm(x, g):
    ms = jnp.mean(x * x, axis=-1, keepdims=True)
    return x * lax.rsqrt(ms + RMS_EPS) * g


def _modulate(x, g, shift, scale):
    return _rms_norm(x, g) * (1.0 + scale) + shift


def _silu(x):
    return x * jax.nn.sigmoid(x)


def _split_bf16(x):
    hi = x.astype(BF16)
    lo = (x - hi.astype(F32)).astype(BF16)
    return hi, lo


def _ordered_after(value, anchor):
    bits = anchor[:SUBLANES_F32, :LANES].astype(jnp.int32)
    zero = lax.shift_right_logical(lax.shift_right_logical(bits, 16), 16).astype(F32)
    head = jnp.concatenate([value[:SUBLANES_F32, :LANES] + zero, value[:SUBLANES_F32, LANES:]], axis=1)
    return jnp.concatenate([head, value[SUBLANES_F32:, :]], axis=0)


def _resident(shape):
    nd = len(shape)
    return pl.BlockSpec(shape, lambda *_: (0,) * nd, pipeline_mode=pl.Buffered(1))


def _mod_spec(layer, which):
    return pl.BlockSpec((None, MOD_ROWS, D_MODEL), lambda *_: (layer, 0, which))


MIX_PRE, MIX_POST, FFN_PRE, FFN_POST = range(4)


def _vec_spec(which, layer):
    return pl.BlockSpec((None, 1, D_MODEL), lambda *_: (which * DEPTH + layer, 0, 0))


def _params():
    return pltpu.CompilerParams(
        dimension_semantics=("arbitrary", "arbitrary"), vmem_limit_bytes=VMEM_LIMIT_BYTES)


MOD_K_ROWS = 256


def _mod_kernel(c_ref, w_ref, b_ref, o_ref):
    k = pl.program_id(1)
    a_hi, a_lo = _split_bf16(_silu(c_ref[...]))
    w_hi, w_lo = _split_bf16(w_ref[...])
    part = _dot(a_hi, w_hi) + _dot(a_hi, w_lo) + _dot(a_lo, w_hi)

    @pl.when(k == 0)
    def _():
        o_ref[...] = part + b_ref[...]

    @pl.when(k > 0)
    def _():
        o_ref[...] += part


def _mod_call(cond, w_mod, b_mod):
    width = N_MOD * D_MODEL
    return pl.pallas_call(
        _mod_kernel,
        grid=(DEPTH, D_MODEL // MOD_K_ROWS),
        in_specs=[
            pl.BlockSpec((MOD_ROWS, MOD_K_ROWS), lambda l, k: (0, k)),
            pl.BlockSpec((None, MOD_K_ROWS, width), lambda l, k: (l, k, 0)),
            pl.BlockSpec((None, 1, width), lambda l, k: (l, 0, 0)),
        ],
        out_specs=pl.BlockSpec((None, MOD_ROWS, width), lambda l, k: (l, 0, 0)),
        out_shape=jax.ShapeDtypeStruct((DEPTH, MOD_ROWS, width), F32),
        compiler_params=_params(),
        name="mod_vectors",
    )(cond, w_mod, b_mod.reshape(DEPTH, 1, width))


def _rope(slab, cos, sin, first_of_pair):
    up = pltpu.roll(slab, LANES - HEAD_DIM // 4, 1)
    dn = pltpu.roll(slab, HEAD_DIM // 4, 1)
    return slab * cos + jnp.where(first_of_pair, up, dn) * sin


def _token_table(rows, col):
    return jnp.concatenate([rows[r:r + 1, :] + col for r in range(rows.shape[0])], axis=0)


INPROJ_SUB_TILES = 4


def _inproj_even_kernel(x_ref, sh_ref, sc_ref, g_ref, w_ref, rcos_ref, rsin_ref, ccos_ref, csin_ref,
                        p_ref, vnorm_ref, ws_ref, bs_ref, q_ref, k_ref, v_ref, ogm_ref, h_ref):
    b = pl.program_id(0)
    shift = sh_ref[pl.ds(b, 1), :]
    scale = sc_ref[pl.ds(b, 1), :]
    n_sub, sub, _ = h_ref.shape
    n_stages = 4
    pieces = _row_pieces(sub, n_stages, 2 * SUBLANES_F32)
    grid_rows = sub // GRID_W

    kv_at, u_at, gv_at = ATTN_DIM, ATTN_DIM + 2 * KV_DIM, IN_DIM_EVEN - GM_DIM
    lane = lax.broadcasted_iota(jnp.int32, (sub, LANES), 1)
    first_of_pair = (lane & (HEAD_DIM // 4)) == 0
    low_half = lax.broadcasted_iota(jnp.int32, (CHUNK, LANES), 1) < HEAD_DIM
    zero = jnp.zeros((CHUNK, LANES), BF16)
    n_chunks = sub // CHUNK

    def pre_norm(t, r0, r1, after=None):
        x = x_ref[t * sub + r0:t * sub + r1, :]
        if after is not None:
            x = _ordered_after(x, after)
        h_ref[t, r0:r1, :] = _modulate(x, g_ref[...], shift, scale).astype(BF16)

    pre_norm(0, 0, sub)
    for t in range(n_sub):
        def next_piece(stage, anchor):
            if t + 1 < n_sub:
                pre_norm(t + 1, *pieces[stage], after=anchor)

        rows_t = slice(t * sub, (t + 1) * sub)
        h = h_ref[t]
        cos = _token_table(rcos_ref[t * grid_rows:(t + 1) * grid_rows, :], ccos_ref[...])
        sin = _token_table(rsin_ref[t * grid_rows:(t + 1) * grid_rows, :], csin_ref[...])

        d = _dot(h, w_ref[:, gv_at:])
        next_piece(0, d)
        q = _dot(h, w_ref[:, :kv_at])
        var = _dot((d * d).astype(BF16), p_ref[...])
        for j in range(ATTN_DIM // LANES):
            cols = slice(j * LANES, (j + 1) * LANES)
            q_ref[rows_t, cols] = _rope(q[:, cols], cos, sin, first_of_pair).astype(BF16)
        next_piece(1, q)
        kv = _dot(h, w_ref[:, kv_at:u_at])
        vn = (d * lax.rsqrt(var + LN_EPS) * vnorm_ref[...]).astype(BF16)
        k_ref[rows_t, :] = _rope(kv[:, :KV_DIM], cos, sin, first_of_pair).astype(BF16)
        v_ref[rows_t, :] = kv[:, KV_DIM:].astype(BF16)
        u = _dot(h, w_ref[:, u_at:gv_at])
        next_piece(2, u)

        for gp in range(GM_GROUPS // 2):
            cols = slice(gp * LANES, (gp + 1) * LANES)
            blocks = []
            for c in range(n_chunks):
                blk = vn[c * CHUNK:(c + 1) * CHUNK, cols]
                blocks.append(jnp.concatenate(
                    [jnp.where(low_half, blk, zero), jnp.where(low_half, zero, blk)], axis=0))
            s = _dot(ws_ref[gp], jnp.concatenate(blocks, axis=1))
            if gp == 0:
                next_piece(3, s)
            for c in range(n_chunks):
                rows = slice(c * CHUNK, (c + 1) * CHUNK)
                gate = s[:, c * LANES:(c + 1) * LANES] + bs_ref[gp]
                ogm_ref[t * sub + c * CHUNK:t * sub + (c + 1) * CHUNK, cols] = (
                    u[rows, cols] * gate).astype(BF16)


def _inproj_even_call(x, mod, gains, w_in, rope, p_mat, v_norm, ws2, bs2, layer):
    B, S, _ = x.shape
    tm = INPROJ_SUB_TILES * TOKENS_PER_STEP
    assert S % tm == 0 and TOKENS_PER_STEP % GRID_W == 0 and (tm // GRID_W) % SUBLANES_F32 == 0
    row_cos, row_sin, col_cos, col_sin = rope
    tok = lambda width: pl.BlockSpec((None, tm, width), lambda b, i: (b, i, 0))
    row_tab = pl.BlockSpec((tm // GRID_W, LANES), lambda b, i: (i, 0))
    return pl.pallas_call(
        _inproj_even_kernel,
        grid=(B, S // tm),
        in_specs=[
            tok(D_MODEL),
            _mod_spec(layer, 0), _mod_spec(layer, 1), _vec_spec(MIX_PRE, layer),
            _resident((D_MODEL, IN_DIM_EVEN)),
            row_tab, row_tab, _resident((GRID_W, LANES)), _resident((GRID_W, LANES)),
            _resident((GM_DIM, GM_DIM)),
            _resident((1, GM_DIM)),
            _resident((GM_GROUPS // 2, CHUNK, 2 * CHUNK)),
            _resident((GM_GROUPS // 2, CHUNK, LANES)),
        ],
        out_specs=[tok(ATTN_DIM), tok(KV_DIM), tok(KV_DIM), tok(GM_DIM)],
        out_shape=[
            jax.ShapeDtypeStruct((B, S, ATTN_DIM), BF16),
            jax.ShapeDtypeStruct((B, S, KV_DIM), BF16),
            jax.ShapeDtypeStruct((B, S, KV_DIM), BF16),
            jax.ShapeDtypeStruct((B, S, GM_DIM), BF16),
        ],
        scratch_shapes=[pltpu.VMEM((INPROJ_SUB_TILES, TOKENS_PER_STEP, D_MODEL), BF16)],
        compiler_params=_params(),
        name="inproj_even",
    )(x, mod, mod, gains, w_in, row_cos, row_sin, col_cos, col_sin, p_mat, v_norm, ws2, bs2)


def _ctx_kv_kernel(ctx_ref, sh_ref, sc_ref, g_ref, w_ref, kc_ref, vc_ref, *, ctx_row):
    shift = sh_ref[ctx_row:ctx_row + 1, :]
    scale = sc_ref[ctx_row:ctx_row + 1, :]
    hc = _modulate(ctx_ref[...], g_ref[...], shift, scale).astype(BF16)
    kv = _dot(hc, w_ref[...])
    kc_ref[...] = kv[:, :KV_DIM].astype(BF16)
    vc_ref[...] = kv[:, KV_DIM:].astype(BF16)


def _ctx_kv_call(ctx, mod, gains, w_kv, layer, ctx_row):
    B, L, _ = ctx.shape
    out = pl.BlockSpec((None, L, KV_DIM), lambda b, i: (b, 0, 0))
    return pl.pallas_call(
        functools.partial(_ctx_kv_kernel, ctx_row=ctx_row),
        grid=(B, 1),
        in_specs=[
            pl.BlockSpec((None, L, D_MODEL), lambda b, i: (b, 0, 0)),
            _mod_spec(layer, 0), _mod_spec(layer, 1), _vec_spec(MIX_PRE, layer),
            _resident((D_MODEL, 2 * KV_DIM)),
        ],
        out_specs=[out, out],
        out_shape=[jax.ShapeDtypeStruct((B, L, KV_DIM), BF16)] * 2,
        compiler_params=_params(),
        name="ctx_kv",
    )(ctx, mod, mod, gains, w_kv)


ATTN_SUB_TILES = 2


N_LATER_WEIGHTS = 5


def _attn_out_kernel(sink_ref, q_ref, kp_ref, km_ref, kn_ref, vp_ref, vm_ref, vn_ref, kc_ref, vc_ref,
                     ogm_ref, x_ref, gt_ref, gpost_ref, wo_ref, *rest):
    later_f32 = rest[:N_LATER_WEIGHTS]
    o_ref = rest[N_LATER_WEIGHTS]
    later_bf16 = rest[N_LATER_WEIGHTS + 1:2 * N_LATER_WEIGHTS + 1]
    oattn_ref, y_ref = rest[-2:]
    for src, dst in zip(later_f32, later_bf16):
        dst[...] = src[...].astype(BF16)

    b = pl.program_id(0)
    i = pl.program_id(1)
    n_steps = pl.num_programs(1)
    tq = q_ref.shape[0]
    n_blocks = tq // BLOCK
    win = 3 * BLOCK

    def window(prev_ref, main_ref, next_ref, qb):
        parts = []
        if qb == 0:
            parts.append(prev_ref[...])
        lo, hi = max(qb - 1, 0) * BLOCK, min(qb + 2, n_blocks) * BLOCK
        parts.append(main_ref[lo:hi, :])
        if qb == n_blocks - 1:
            parts.append(next_ref[...])
        return parts
    kc = kc_ref[...]
    vc = vc_ref[...]
    n_keys = win + kc.shape[0]

    qi = lax.broadcasted_iota(jnp.int32, (BLOCK, win), 0)
    kj = lax.broadcasted_iota(jnp.int32, (BLOCK, win), 1)
    band = jnp.where(jnp.abs(kj - BLOCK - qi) <= WINDOW, 0.0, NEG_INF).astype(F32)
    before_start = jnp.where(kj < BLOCK, jnp.where(i == 0, NEG_INF, 0.0), 0.0).astype(F32)
    after_end = jnp.where(kj >= 2 * BLOCK, jnp.where(i == n_steps - 1, NEG_INF, 0.0), 0.0).astype(F32)

    q_half = [(lax.broadcasted_iota(jnp.int32, (BLOCK, LANES), 1) // HEAD_DIM) == j
              for j in range(KV_HEADS)]
    kv_half = [(lax.broadcasted_iota(jnp.int32, (n_keys, LANES), 1) // HEAD_DIM) == j
               for j in range(KV_HEADS)]

    zero = jnp.zeros((), BF16)
    ones_half = [jnp.where(kv_half[j], 1.0, 0.0).astype(BF16) for j in range(KV_HEADS)]

    def scores(qb):
        q_blk = q_ref[qb * BLOCK:(qb + 1) * BLOCK, :]
        qs = jnp.concatenate([q_blk[:, g * LANES:(g + 1) * LANES] for g in range(GQA)], axis=0)
        k_all = jnp.concatenate(window(kp_ref, km_ref, kn_ref, qb) + [kc], axis=0)
        k_heads = jnp.concatenate([jnp.where(kv_half[j], k_all, zero) for j in range(KV_HEADS)], axis=0)
        return _dot_nt(qs, k_heads)

    n_sub, sub, _ = y_ref.shape
    blocks_per_sub = sub // BLOCK
    post_pieces = _row_pieces(sub, blocks_per_sub, SUBLANES_F32)
    gate = gt_ref[pl.ds(b, 1), :]

    def post_norm(t, r0, r1):
        rows = slice(t * sub + r0, t * sub + r1)
        out = x_ref[rows, :] + gate * _rms_norm(y_ref[t, r0:r1, :], gpost_ref[...])
        o_ref[rows, :] = out
        return out

    done = None
    s_next = scores(0)
    for qb in range(n_blocks):
        t, qq = divmod(qb, blocks_per_sub)
        rows = slice(qb * BLOCK, (qb + 1) * BLOCK)
        bias = band
        if qb == 0:
            bias = bias + before_start
        if qb == n_blocks - 1:
            bias = bias + after_end
        s = s_next
        if done is not None:
            s = _ordered_after(s, done)
            done = None
        if qb + 1 < n_blocks:
            s_next = scores(qb + 1)

        probs, sink_terms = [], []
        for g in range(GQA):
            p_g, t_g = [], []
            for j in range(KV_HEADS):
                sink = sink_ref[0, j * GQA + g] * LOG2_E
                sg = s[g * BLOCK:(g + 1) * BLOCK, j * n_keys:(j + 1) * n_keys]
                sg = jnp.concatenate(
                    [sg[:, :BLOCK] + bias[:, :BLOCK], sg[:, BLOCK:2 * BLOCK],
                     sg[:, 2 * BLOCK:win] + bias[:, 2 * BLOCK:], sg[:, win:]], axis=1)
                m = jnp.maximum(jnp.max(sg, axis=-1, keepdims=True), sink)
                p_g.append(jnp.exp2(sg - m).astype(BF16))
                t_g.append(jnp.broadcast_to(jnp.exp2(sink - m), (BLOCK, LANES)))
            probs.append(jnp.concatenate(p_g, axis=1))
            sink_terms.append(jnp.where(q_half[0], t_g[0], t_g[1]))

        v_all = jnp.concatenate(window(vp_ref, vm_ref, vn_ref, qb) + [vc], axis=0)
        v_heads = jnp.concatenate(
            [jnp.concatenate([jnp.where(kv_half[j], v_all, zero), ones_half[j]], axis=1)
             for j in range(KV_HEADS)], axis=0)
        e = _dot(jnp.concatenate(probs, axis=0), v_heads)
        for g in range(GQA):
            eg = e[g * BLOCK:(g + 1) * BLOCK, :]
            slab = eg[:, :LANES] / (eg[:, LANES:] + sink_terms[g])
            oattn_ref[rows, g * LANES:(g + 1) * LANES] = slab.astype(BF16)

        if t > 0:
            done = post_norm(t - 1, *post_pieces[qq])
        if qq == blocks_per_sub - 1:
            sub_rows = slice(t * sub, (t + 1) * sub)
            y_ref[t] = (_dot(oattn_ref[sub_rows, :], wo_ref[:ATTN_DIM, :])
                        + _dot(ogm_ref[sub_rows, :], wo_ref[ATTN_DIM:, :]))
    post_norm(n_sub - 1, 0, sub)


def _attn_out_call(sink, q, k, v, kc, vc, ogm, x, mod, gains, w_out, later_weights, layer):
    B, S, _ = x.shape
    tq = ATTN_SUB_TILES * TOKENS_PER_STEP
    assert S % tq == 0
    n_steps = B * (S // tq)
    assert len(later_weights) == N_LATER_WEIGHTS
    slices = []
    for w in later_weights:
        rows = w.shape[0] // n_steps
        assert rows * n_steps == w.shape[0] and rows % (2 * SUBLANES_F32) == 0
        slices.append(pl.BlockSpec((rows, w.shape[1]), lambda b, i: (b * (S // tq) + i, 0)))
    bpt = tq // BLOCK
    n_blk = S // BLOCK
    L = kc.shape[1]
    tok = lambda width: pl.BlockSpec((None, tq, width), lambda b, i: (b, i, 0))
    prev = pl.BlockSpec((None, BLOCK, KV_DIM), lambda b, i: (b, jnp.maximum(i * bpt - 1, 0), 0))
    nxt = pl.BlockSpec((None, BLOCK, KV_DIM),
                       lambda b, i: (b, jnp.minimum((i + 1) * bpt, n_blk - 1), 0))
    ctx = pl.BlockSpec((None, L, KV_DIM), lambda b, i: (b, 0, 0))
    outs = pl.pallas_call(
        _attn_out_kernel,
        grid=(B, S // tq),
        in_specs=[
            pl.BlockSpec(memory_space=pltpu.SMEM),
            tok(ATTN_DIM),
            prev, tok(KV_DIM), nxt,
            prev, tok(KV_DIM), nxt,
            ctx, ctx,
            tok(GM_DIM), tok(D_MODEL),
            _mod_spec(layer, 2), _vec_spec(MIX_POST, layer),
            _resident((ATTN_DIM + GM_DIM, D_MODEL)),
        ] + slices,
        out_specs=[tok(D_MODEL)] + slices,
        out_shape=[jax.ShapeDtypeStruct((B, S, D_MODEL), F32)]
        + [jax.ShapeDtypeStruct(w.shape, BF16) for w in later_weights],
        scratch_shapes=[pltpu.VMEM((tq, ATTN_DIM), BF16),
                        pltpu.VMEM((ATTN_SUB_TILES, TOKENS_PER_STEP, D_MODEL), F32)],
        compiler_params=_params(),
        name="attn_out",
    )(sink, q, k, k, k, v, v, v, kc, vc, ogm, x, mod, gains, w_out, *later_weights)
    return outs[0], outs[1:]


def _row_pieces(n_rows, n_pieces, align):
    units = n_rows // align
    assert units * align == n_rows and units >= n_pieces
    bounds = [align * ((units * p) // n_pieces) for p in range(n_pieces + 1)]
    return list(zip(bounds[:-1], bounds[1:]))


def _ffn_kernel(x_ref, sh_ref, sc_ref, gt_ref, gpre_ref, gpost_ref, w1_ref, w3_ref, w2_ref, o_ref,
                hf_ref, acc_ref, g_ref):
    b = pl.program_id(0)
    shift, scale, gate = sh_ref[pl.ds(b, 1), :], sc_ref[pl.ds(b, 1), :], gt_ref[pl.ds(b, 1), :]
    sizes = FFN_SUB_ROWS
    starts = [sum(sizes[:t]) for t in range(len(sizes))]
    n_sub = len(sizes)
    n_chunks = D_FF // FF_CHUNK
    pieces = [_row_pieces(sz, n_chunks, 2 * SUBLANES_F32) for sz in sizes]

    def pre_norm(t, r0, r1, after=None):
        x = x_ref[starts[t] + r0:starts[t] + r1, :]
        if after is not None:
            x = _ordered_after(x, after)
        hf_ref[t, r0:r1, :] = _modulate(x, gpre_ref[...], shift, scale).astype(BF16)

    def post_norm(t, r0, r1):
        rows = slice(starts[t] + r0, starts[t] + r1)
        out = x_ref[rows, :] + gate * _rms_norm(acc_ref[t, r0:r1, :], gpost_ref[...])
        o_ref[rows, :] = out
        return out

    pre_norm(0, 0, sizes[0])
    for t in range(n_sub):
        sz = sizes[t]
        done = None
        for c in range(n_chunks):
            cols = slice(c * FF_CHUNK, (c + 1) * FF_CHUNK)
            hf = hf_ref[t, :sz, :]
            up = _dot(hf, w1_ref[:, cols])
            if done is not None:
                up = _ordered_after(up, done)
            g_ref[t, :sz, cols] = (_silu(up) * _dot(hf, w3_ref[:, cols])).astype(BF16)
            if t + 1 < n_sub:
                pre_norm(t + 1, *pieces[t + 1][c], after=up)
            if t > 0:
                done = post_norm(t - 1, *pieces[t - 1][c])
        acc_ref[t, :sz, :] = _dot(g_ref[t, :sz, :], w2_ref[...])
    post_norm(n_sub - 1, 0, sizes[-1])


def _ffn_call(x, mod, gains, w1, w3, w2, layer):
    B, S, _ = x.shape
    tm = sum(FFN_SUB_ROWS)
    assert S % tm == 0
    tok = pl.BlockSpec((None, tm, D_MODEL), lambda b, i: (b, i, 0))
    slot = (len(FFN_SUB_ROWS), max(FFN_SUB_ROWS))
    return pl.pallas_call(
        _ffn_kernel,
        grid=(B, S // tm),
        in_specs=[
            tok,
            _mod_spec(layer, 3), _mod_spec(layer, 4), _mod_spec(layer, 5),
            _vec_spec(FFN_PRE, layer), _vec_spec(FFN_POST, layer),
            pl.BlockSpec((None, D_MODEL, D_FF), lambda b, i: (layer, 0, 0), pipeline_mode=pl.Buffered(1)),
            pl.BlockSpec((None, D_MODEL, D_FF), lambda b, i: (layer, 0, 0), pipeline_mode=pl.Buffered(1)),
            pl.BlockSpec((None, D_FF, D_MODEL), lambda b, i: (layer, 0, 0), pipeline_mode=pl.Buffered(1)),
        ],
        out_specs=tok,
        out_shape=jax.ShapeDtypeStruct((B, S, D_MODEL), F32),
        scratch_shapes=[pltpu.VMEM(slot + (D_MODEL,), BF16), pltpu.VMEM(slot + (D_MODEL,), F32),
                        pltpu.VMEM(slot + (D_FF,), BF16)],
        compiler_params=_params(),
        name="swiglu_ffn",
    )(x, mod, mod, mod, gains, gains, w1, w3, w2)


HALO = SUBLANES_F32

CONV_SUB_ROWS = (512, 512)


def _conv_mixer_kernel(xp_ref, x_ref, xn_ref, sh_ref, sc_ref, gt_ref, gpre_ref, gpost_ref, wi_ref,
                       cw_ref, wo_ref, o_ref, h_ref, acc_ref, y_ref):
    b = pl.program_id(0)
    i = pl.program_id(1)
    n_steps = pl.num_programs(1)
    shift, scale, gate = sh_ref[pl.ds(b, 1), :], sc_ref[pl.ds(b, 1), :], gt_ref[pl.ds(b, 1), :]
    tm = x_ref.shape[0]
    sizes = CONV_SUB_ROWS
    starts = [sum(sizes[:t]) for t in range(len(sizes))]
    n_sub = len(sizes)
    exts = [sz + 2 * HALO for sz in sizes]
    pre_pieces = [_row_pieces(e, 3, 2 * SUBLANES_F32) for e in exts]

    def ext_rows(t, r0, r1):
        lo, hi = starts[t] + r0 - HALO, starts[t] + r1 - HALO
        parts = [x_ref[max(lo, 0):min(hi, tm), :]]
        if lo < 0:
            assert lo == -HALO
            parts.insert(0, xp_ref[...])
        if hi > tm:
            assert hi == tm + HALO
            parts.append(xn_ref[...])
        return parts[0] if len(parts) == 1 else jnp.concatenate(parts, axis=0)

    def pre_norm(t, r0, r1, after=None):
        x = ext_rows(t, r0, r1)
        if after is not None:
            x = _ordered_after(x, after)
        h_ref[t, r0:r1, :] = _modulate(x, gpre_ref[...], shift, scale).astype(BF16)

    def post_norm(t):
        rows = slice(starts[t], starts[t] + sizes[t])
        out = x_ref[rows, :] + gate * _rms_norm(acc_ref[t, :sizes[t], :], gpost_ref[...])
        o_ref[rows, :] = out

    first_row = jnp.where(i > 0, 0, HALO)
    end_row = jnp.where(i < n_steps - 1, HALO, 0)

    def project(t):
        outs = []
        for k in range(3):
            if t + 1 < n_sub:
                pre_norm(t + 1, *pre_pieces[t + 1][k], after=outs[-1] if outs else None)
            outs.append(_dot(h_ref[t, :exts[t], :], wi_ref[:, k * D_MODEL:(k + 1) * D_MODEL]))
        return outs

    def mix(t, bg, cg, hx):
        sub, ext = sizes[t], exts[t]
        y_ref[t, :ext, :] = cg * hx
        if t == 0:
            keep = lax.broadcasted_iota(jnp.int32, (HALO, 1), 0) >= first_row
            y_ref[t, :HALO, :] = jnp.where(keep, y_ref[t, :HALO, :], 0.0)
        if t == n_sub - 1:
            keep = lax.broadcasted_iota(jnp.int32, (HALO, 1), 0) < end_row
            y_ref[t, ext - HALO:ext, :] = jnp.where(keep, y_ref[t, ext - HALO:ext, :], 0.0)
        cw = cw_ref[...]
        yconv = (cw[0:1, :] * y_ref[t, HALO - 1:HALO - 1 + sub, :]
                 + cw[1:2, :] * y_ref[t, HALO:HALO + sub, :]
                 + cw[2:3, :] * y_ref[t, HALO + 1:HALO + 1 + sub, :])
        return (bg[HALO:HALO + sub, :] * yconv).astype(BF16)

    pre_norm(0, 0, exts[0])
    projected = project(0)
    for t in range(n_sub):
        current = projected
        if t + 1 < n_sub:
            projected = project(t + 1)
        z = mix(t, *current)
        if t > 0:
            post_norm(t - 1)
        acc_ref[t, :sizes[t], :] = _dot(z, wo_ref[...])
    post_norm(n_sub - 1)


def _conv_mixer_call(x, mod, gains, w_in, conv_w, w_out, layer):
    B, S, _ = x.shape
    tm = sum(CONV_SUB_ROWS)
    assert S % tm == 0
    slot = (len(CONV_SUB_ROWS), max(CONV_SUB_ROWS))
    hpt = tm // HALO
    n_halo = S // HALO
    tok = pl.BlockSpec((None, tm, D_MODEL), lambda b, i: (b, i, 0))
    prev = pl.BlockSpec((None, HALO, D_MODEL), lambda b, i: (b, jnp.maximum(i * hpt - 1, 0), 0))
    nxt = pl.BlockSpec((None, HALO, D_MODEL),
                       lambda b, i: (b, jnp.minimum((i + 1) * hpt, n_halo - 1), 0))
    return pl.pallas_call(
        _conv_mixer_kernel,
        grid=(B, S // tm),
        in_specs=[
            prev, tok, nxt,
            _mod_spec(layer, 0), _mod_spec(layer, 1), _mod_spec(layer, 2),
            _vec_spec(MIX_PRE, layer), _vec_spec(MIX_POST, layer),
            _resident((D_MODEL, 3 * D_MODEL)),
            _resident((CONV_WIDTH, D_MODEL)),
            _resident((D_MODEL, D_MODEL)),
        ],
        out_specs=tok,
        out_shape=jax.ShapeDtypeStruct((B, S, D_MODEL), F32),
        scratch_shapes=[pltpu.VMEM((slot[0], slot[1] + 2 * HALO, D_MODEL), BF16),
                        pltpu.VMEM(slot + (D_MODEL,), F32),
                        pltpu.VMEM((slot[0], slot[1] + 2 * HALO, D_MODEL), F32)],
        compiler_params=_params(),
        name="conv_mixer",
    )(x, x, x, mod, mod, mod, gains, gains, w_in, conv_w, w_out)


def _rope_tables(seq):
    quarter = HEAD_DIM // 4
    rows = seq // GRID_W
    inv = ROPE_THETA ** (-jnp.arange(quarter, dtype=F32) / quarter)
    ra = jnp.arange(rows, dtype=F32)[:, None] * inv
    ca = jnp.arange(GRID_W, dtype=F32)[:, None] * inv
    reps = LANES // HEAD_DIM
    zr, zc = jnp.zeros_like(ra), jnp.zeros_like(ca)
    lanes = lambda parts: jnp.tile(jnp.concatenate(parts, axis=-1), (1, reps))
    row_cos = lanes([jnp.cos(ra), jnp.cos(ra), zr, zr])
    row_sin = lanes([-jnp.sin(ra), jnp.sin(ra), zr, zr])
    col_cos = lanes([zc, zc, jnp.cos(ca), jnp.cos(ca)])
    col_sin = lanes([zc, zc, -jnp.sin(ca), jnp.sin(ca)])
    return row_cos, row_sin, col_cos, col_sin


def _to_head_slabs(w, axis):
    shape = w.shape
    split = shape[:axis] + (KV_HEADS, GQA, HEAD_DIM) + shape[axis + 1:]
    return jnp.swapaxes(w.reshape(split), axis, axis + 1).reshape(shape)


def _prep_even_weights(w_in, ws, bs):
    wq = _to_head_slabs(w_in[:, :ATTN_DIM], 1) * (HEAD_DIM ** -0.5 * LOG2_E)
    rest = w_in[:, ATTN_DIM:ATTN_DIM + 2 * KV_DIM + GM_DIM]
    wg = w_in[:, IN_DIM_EVEN - GM_DIM:].reshape(D_MODEL, GM_GROUPS, HEAD_DIM)
    wg = (wg - jnp.mean(wg, axis=-1, keepdims=True)).reshape(D_MODEL, GM_DIM)
    w_all = jnp.concatenate([wq, rest, wg], axis=1).astype(BF16)
    w_kv = w_in[:, ATTN_DIM:ATTN_DIM + 2 * KV_DIM].astype(BF16)
    half = GM_GROUPS // 2
    ws2 = ws.reshape(half, 2, CHUNK, CHUNK).transpose(0, 2, 1, 3).reshape(half, CHUNK, 2 * CHUNK)
    bs2 = jnp.repeat(bs.reshape(half, 2, CHUNK).transpose(0, 2, 1), HEAD_DIM, axis=-1)
    return w_all, w_kv, ws2.astype(BF16), bs2.astype(F32)


def kernel(x, c, ctx, c_ctx, w_mod, b_mod, g_mix_pre, g_mix_post, g_ffn_pre, g_ffn_post, ffn_w1, ffn_w3,
           ffn_w2, a_w_in, a_sink, gm_v_norm, gm_ws, gm_bs, a_w_out, sc_w_in, sc_conv, sc_w_out):
    B, S, D = x.shape
    assert D == D_MODEL and S % TOKENS_PER_STEP == 0 and B + 1 <= MOD_ROWS
    assert a_w_in.shape[0] == 1 and sc_w_in.shape[0] == 1 and w_mod.shape[0] == DEPTH

    cond = jnp.concatenate([c, c_ctx[None, :], jnp.zeros((MOD_ROWS - B - 1, D), F32)], axis=0)
    mod = _mod_call(cond, w_mod, b_mod)

    gains = jnp.stack([g_mix_pre, g_mix_post, g_ffn_pre, g_ffn_post]).reshape(4 * DEPTH, 1, D)
    later = (ffn_w1.reshape(DEPTH * D, D_FF), ffn_w3.reshape(DEPTH * D, D_FF),
             ffn_w2.reshape(DEPTH * D_FF, D), sc_w_in[0], sc_w_out[0])

    w_all, w_kv, ws2, bs2 = _prep_even_weights(a_w_in[0], gm_ws[0], gm_bs[0])
    group = jnp.arange(GM_DIM) // HEAD_DIM
    p_mat = jnp.where(group[:, None] == group[None, :], 1.0 / HEAD_DIM, 0.0).astype(BF16)
    q, k, v, ogm = _inproj_even_call(
        x, mod, gains, w_all, _rope_tables(S), p_mat, gm_v_norm, ws2, bs2, layer=0)
    kc, vc = _ctx_kv_call(ctx, mod, gains, w_kv, layer=0, ctx_row=B)
    w_out = jnp.concatenate([_to_head_slabs(a_w_out[0][:ATTN_DIM], 0), a_w_out[0][ATTN_DIM:]], axis=0)
    x, later = _attn_out_call(a_sink, q, k, v, kc, vc, ogm, x, mod, gains, w_out.astype(BF16), later,
                              layer=0)
    w1, w3 = later[0].reshape(DEPTH, D, D_FF), later[1].reshape(DEPTH, D, D_FF)
    w2, sc_in, sc_out = later[2].reshape(DEPTH, D_FF, D), later[3], later[4]
    x = _ffn_call(x, mod, gains, w1, w3, w2, layer=0)

    x = _conv_mixer_call(x, mod, gains, sc_in, sc_conv[0], sc_out, layer=1)
    x = _ffn_call(x, mod, gains, w1, w3, w2, layer=1)
    return x
```
